```python
import math
import jax, jax.numpy as jnp
from jax import lax
import numpy as np

D_MODEL = 2048
BATCH = 1
SEQ = 16384
DEPTH = 2
DEC_BATCH = 32
DEC_SEQ = 32
PAST_LEN = 1024

CHUNK = 64
N_EVEN = (DEPTH + 1) // 2
N_ODD = DEPTH // 2
S5_WIDTH = D_MODEL // 2
S5_GROUP = 16
S5_GROUPS = S5_WIDTH // S5_GROUP
S5_STATE = 64
S5_BLOCK = 128
ATT_WIDTH = D_MODEL // 2
ATT_VDIM = 128
ATT_HEADS = ATT_WIDTH // ATT_VDIM
QK_DIM = ATT_VDIM // 2
Q_BLOCK = 128
CONV_DIM = D_MODEL
CONV_W = 31
EVEN_IN = 2 * S5_WIDTH + 4 * ATT_WIDTH
EPS = 1e-6
NEG_INF = -1e30

kernel_name = "streaming_s5_diffattn_conformer_step"


def rms_norm(x, g):
    xf = x.astype(jnp.float32)
    y = xf * lax.rsqrt(jnp.mean(xf * xf, axis=-1, keepdims=True) + EPS)
    return (y * g.astype(jnp.float32)).astype(x.dtype)


def layer_norm(x, g, b):
    xf = x.astype(jnp.float32)
    mu = jnp.mean(xf, axis=-1, keepdims=True)
    xc = xf - mu
    y = xc * lax.rsqrt(jnp.mean(xc * xc, axis=-1, keepdims=True) + EPS)
    return (y * g.astype(jnp.float32) + b.astype(jnp.float32)).astype(x.dtype)


def qk_norm(x, g):
    shp = x.shape
    xf = x.astype(jnp.float32).reshape(shp[:-1] + (2, QK_DIM))
    xf = xf * lax.rsqrt(jnp.mean(xf * xf, axis=-1, keepdims=True) + EPS)
    return (xf.reshape(shp) * g.astype(jnp.float32)).astype(x.dtype)


def alibi_slopes():
    return 2.0 ** (-8.0 * jnp.arange(1, ATT_HEADS + 1, dtype=jnp.float32) / ATT_HEADS)


def complex_combine(e1, e2):
    a1r, a1i, b1r, b1i = e1
    a2r, a2i, b2r, b2i = e2
    ar = a1r * a2r - a1i * a2i
    ai = a1r * a2i + a1i * a2r
    br = a2r * b1r - a2i * b1i + b2r
    bi = a2r * b1i + a2i * b1r + b2i
    return (ar, ai, br, bi)


def s5_scan(u, h0r, h0i, lam_re, lam_im, log_dt, b_re, b_im, c_re, c_im, d_skip):
    f32 = jnp.float32
    bsz, L, _ = u.shape
    uf = u.astype(f32).reshape(bsz, L, S5_GROUPS, S5_GROUP)
    lr, li = lam_re.astype(f32), lam_im.astype(f32)
    dt = jnp.exp(log_dt.astype(f32))[:, None]
    mag = jnp.exp(lr * dt)
    ar = mag * jnp.cos(li * dt)
    ai = mag * jnp.sin(li * dt)
    den = lr * lr + li * li
    zr = ((ar - 1.0) * lr + ai * li) / den
    zi = (ai * lr - (ar - 1.0) * li) / den
    br, bi = b_re.astype(f32), b_im.astype(f32)
    bbr = zr[..., None] * br - zi[..., None] * bi
    bbi = zr[..., None] * bi + zi[..., None] * br
    cr, ci = c_re.astype(f32), c_im.astype(f32)
    d = d_skip.astype(f32)
    blk = S5_BLOCK if L % S5_BLOCK == 0 else L
    nb = L // blk
    ub = uf.reshape(bsz, nb, blk, S5_GROUPS, S5_GROUP).swapaxes(0, 1)

    def step(carry, u_blk):
        hr, hi = carry
        xr_in = jnp.einsum('btgh,gph->btgp', u_blk, bbr)
        xi_in = jnp.einsum('btgh,gph->btgp', u_blk, bbi)
        a_r = jnp.broadcast_to(ar, xr_in.shape)
        a_i = jnp.broadcast_to(ai, xr_in.shape)
        A_r, A_i, X_r, X_i = lax.associative_scan(complex_combine, (a_r, a_i, xr_in, xi_in), axis=1)
        xr = A_r * hr[:, None] - A_i * hi[:, None] + X_r
        xi = A_r * hi[:, None] + A_i * hr[:, None] + X_i
        y = (jnp.einsum('btgp,ghp->btgh', xr, cr) - jnp.einsum('btgp,ghp->btgh', xi, ci)
             + d * u_blk)
        return (xr[:, -1], xi[:, -1]), y

    (hr, hi), yb = lax.scan(step, (h0r.astype(f32), h0i.astype(f32)), ub)
    y = yb.swapaxes(0, 1).reshape(bsz, L, S5_WIDTH).astype(u.dtype)
    return y, hr, hi


def diff_attn_core(q, k, v, q_pos, k_pos, lam):
    f32 = jnp.float32
    scale = QK_DIM ** -0.5
    q1, q2 = q[..., :QK_DIM], q[..., QK_DIM:]
    k1, k2 = k[..., :QK_DIM], k[..., QK_DIM:]
    dist = jnp.abs(q_pos[:, None] - k_pos[None, :]).astype(f32)
    vis = (k_pos[None, :] // CHUNK) <= (q_pos[:, None] // CHUNK)
    bias = jnp.where(vis[None], -alibi_slopes()[:, None, None] * dist[None], NEG_INF)
    s1 = jnp.einsum('bqhd,bkhd->bhqk', q1, k1).astype(f32) * scale + bias
    s2 = jnp.einsum('bqhd,bkhd->bhqk', q2, k2).astype(f32) * scale + bias
    w = (jax.nn.softmax(s1, axis=-1) - lam * jax.nn.softmax(s2, axis=-1)).astype(v.dtype)
    return jnp.einsum('bhqk,bkhd->bqhd', w, v)


def setup_inputs(seed: int = 0) -> dict:
    key = jax.random.key(seed)
    ks = iter(jax.random.split(key, 48))
    f32 = jnp.float32
    D = D_MODEL

    def nrm(shape, s=1.0):
        return s * jax.random.normal(next(ks), shape, f32)

    n_idx = jnp.arange(S5_STATE, dtype=f32)
    return {
        "x_prompt": nrm((BATCH, SEQ, D)),
        "x_sample": nrm((DEC_BATCH, DEC_SEQ, D)),
        "c_prompt": nrm((BATCH, D)),
        "c_sample": nrm((DEC_BATCH, D)),
        "cache_k": nrm((N_EVEN, DEC_BATCH, PAST_LEN, ATT_HEADS, ATT_VDIM)),
        "cache_v": nrm((N_EVEN, DEC_BATCH, PAST_LEN, ATT_HEADS, ATT_VDIM)),
        "state_s5_re": nrm((N_EVEN, DEC_BATCH, S5_GROUPS, S5_STATE), 0.1),
        "state_s5_im": nrm((N_EVEN, DEC_BATCH, S5_GROUPS, S5_STATE), 0.1),
        "state_conv": nrm((N_ODD, DEC_BATCH, CONV_W - 1, CONV_DIM), 0.5),
        "norm_g": 1.0 + nrm((DEPTH, D), 0.05),
        "w_ada": nrm((DEPTH, D, 3 * D), 0.5 * D ** -0.5),
        "b_ada": nrm((DEPTH, 3 * D), 0.02),
        "w_in_even": nrm((N_EVEN, D, EVEN_IN), D ** -0.5),
        "w_out_even": nrm((N_EVEN, S5_WIDTH + ATT_WIDTH, D), (S5_WIDTH + ATT_WIDTH) ** -0.5),
        "s5_lam_re": -0.5 + nrm((N_EVEN, S5_GROUPS, S5_STATE), 0.01),
        "s5_lam_im": math.pi * n_idx + nrm((N_EVEN, S5_GROUPS, S5_STATE), 0.01),
        "s5_log_dt": jax.random.uniform(next(ks), (N_EVEN, S5_GROUPS), f32, math.log(1e-3), math.log(1e-1)),
        "s5_b_re": nrm((N_EVEN, S5_GROUPS, S5_STATE, S5_GROUP), (2 * S5_GROUP) ** -0.5),
        "s5_b_im": nrm((N_EVEN, S5_GROUPS, S5_STATE, S5_GROUP), (2 * S5_GROUP) ** -0.5),
        "s5_c_re": nrm((N_EVEN, S5_GROUPS, S5_GROUP, S5_STATE), S5_STATE ** -0.5),
        "s5_c_im": nrm((N_EVEN, S5_GROUPS, S5_GROUP, S5_STATE), S5_STATE ** -0.5),
        "s5_d": nrm((N_EVEN, S5_GROUPS, S5_GROUP)),
        "s5_w_glu": nrm((N_EVEN, S5_WIDTH, 2 * S5_WIDTH), S5_WIDTH ** -0.5),
        "q_norm_g": 1.0 + nrm((N_EVEN, ATT_VDIM), 0.05),
        "k_norm_g": 1.0 + nrm((N_EVEN, ATT_VDIM), 0.05),
        "lam_q1": nrm((N_EVEN, QK_DIM), 0.1),
        "lam_k1": nrm((N_EVEN, QK_DIM), 0.1),
        "lam_q2": nrm((N_EVEN, QK_DIM), 0.1),
        "lam_k2": nrm((N_EVEN, QK_DIM), 0.1),
        "attn_out_g": 1.0 + nrm((N_EVEN, ATT_VDIM), 0.05),
        "w_in_odd": nrm((N_ODD, D, 3 * CONV_DIM), D ** -0.5),
        "conv_w": nrm((N_ODD, CONV_W, CONV_DIM), CONV_W ** -0.5),
        "conv_b": nrm((N_ODD, CONV_DIM), 0.02),
        "conv_ln_g": 1.0 + nrm((N_ODD, CONV_DIM), 0.05),
        "conv_ln_b": nrm((N_ODD, CONV_DIM), 0.02),
        "w_out_odd": nrm((N_ODD, CONV_DIM, D), CONV_DIM ** -0.5),
    }


def reference(x_prompt, x_sample, c_prompt, c_sample, cache_k, cache_v, state_s5_re, state_s5_im,
              state_conv, norm_g, w_ada, b_ada, w_in_even, w_out_even, s5_lam_re, s5_lam_im,
              s5_log_dt, s5_b_re, s5_b_im, s5_c_re, s5_c_im, s5_d, s5_w_glu, q_norm_g, k_norm_g,
              lam_q1, lam_k1, lam_q2, lam_k2, attn_out_g, w_in_odd, conv_w, conv_b, conv_ln_g,
              conv_ln_b, w_out_odd):
    f32 = jnp.float32
    past = cache_k.shape[2]

    def modulate(x, c, l):
        mod = jax.nn.silu(c) @ w_ada[l] + b_ada[l]
        shift, scale, gate = jnp.split(mod, 3, axis=-1)
        h = rms_norm(x, norm_g[l]) * (1.0 + scale[:, None]) + shift[:, None]
        return h, gate[:, None]

    def even_mixer(h, l, h0r, h0i, k_past, v_past, pos0):
        i = l // 2
        bsz, L, _ = h.shape
        sw, aw = S5_WIDTH, ATT_WIDTH
        proj = h @ w_in_even[i]
        u, zs, q, k, v, za = jnp.split(proj, [sw, 2 * sw, 2 * sw + aw, 2 * sw + 2 * aw, 2 * sw + 3 * aw], axis=-1)
        y, hr, hi = s5_scan(u, h0r, h0i, s5_lam_re[i], s5_lam_im[i], s5_log_dt[i], s5_b_re[i], s5_b_im[i],
                            s5_c_re[i], s5_c_im[i], s5_d[i])
        ga, gb = jnp.split(jax.nn.gelu(y) @ s5_w_glu[i], 2, axis=-1)
        s5_out = ga * jax.nn.sigmoid(gb) * jax.nn.silu(zs)
        q = qk_norm(q.reshape(bsz, L, ATT_HEADS, ATT_VDIM), q_norm_g[i])
        k = qk_norm(k.reshape(bsz, L, ATT_HEADS, ATT_VDIM), k_norm_g[i])
        v = v.reshape(bsz, L, ATT_HEADS, ATT_VDIM)
        lam_init = 0.8 - 0.6 * math.exp(-0.3 * l)
        lam = (jnp.exp(jnp.sum(lam_q1[i].astype(f32) * lam_k1[i].astype(f32)))
               - jnp.exp(jnp.sum(lam_q2[i].astype(f32) * lam_k2[i].astype(f32))) + lam_init)
        if k_past is None:
            nb = L // Q_BLOCK
            k_pos = jnp.arange(L)
            qb = q.reshape(bsz, nb, Q_BLOCK, ATT_HEADS, ATT_VDIM).swapaxes(0, 1)
            pb = jnp.arange(L).reshape(nb, Q_BLOCK)
            ob = lax.map(lambda a: diff_attn_core(a[0], k, v, a[1], k_pos, lam), (qb, pb))
            o = ob.swapaxes(0, 1).reshape(bsz, L, ATT_HEADS, ATT_VDIM)
        else:
            k_all = jnp.concatenate([k_past.astype(k.dtype), k], axis=1)
            v_all = jnp.concatenate([v_past.astype(v.dtype), v], axis=1)
            k_pos = jnp.arange(k_all.shape[1])
            q_pos = pos0 + jnp.arange(L)
            o = diff_attn_core(q, k_all, v_all, q_pos, k_pos, lam)
        o = rms_norm(o, attn_out_g[i]) * (1.0 - lam_init)
        att_out = o.reshape(bsz, L, ATT_WIDTH) * jax.nn.silu(za)
        out = jnp.concatenate([s5_out, att_out], axis=-1) @ w_out_even[i]
        return out, k, v, hr, hi

    def odd_mixer(h, l, buf):
        i = l // 2
        a, b, z = jnp.split(h @ w_in_odd[i], 3, axis=-1)
        g = a * jax.nn.sigmoid(b)
        if buf is None:
            buf = jnp.zeros((h.shape[0], CONV_W - 1, CONV_DIM), g.dtype)
        xp = jnp.concatenate([buf.astype(g.dtype), g], axis=1)
        y = lax.conv_general_dilated(xp, conv_w[i][:, None, :].astype(xp.dtype), (1,), 'VALID',
                                     dimension_numbers=('NWC', 'WIO', 'NWC'),
                                     feature_group_count=CONV_DIM) + conv_b[i]
        y = jax.nn.silu(layer_norm(y, conv_ln_g[i], conv_ln_b[i])) * jax.nn.silu(z)
        return y @ w_out_odd[i], xp[:, -(CONV_W - 1):]

    yp, ys = x_prompt, x_sample
    kp, vp, srp, sip, cvp = [], [], [], [], []
    kn, vn, srs, sis, cvs = [], [], [], [], []
    for l in range(DEPTH):
        hp, gp = modulate(yp, c_prompt, l)
        hs, gs = modulate(ys, c_sample, l)
        if l % 2 == 0:
            i = l // 2
            z0 = jnp.zeros((yp.shape[0], S5_GROUPS, S5_STATE), f32)
            op, k_, v_, hr, hi = even_mixer(hp, l, z0, z0, None, None, 0)
            kp.append(k_); vp.append(v_); srp.append(hr); sip.append(hi)
            os_, k_, v_, hr, hi = even_mixer(hs, l, state_s5_re[i], state_s5_im[i], cache_k[i], cache_v[i], past)
            kn.append(k_); vn.append(v_); srs.append(hr); sis.append(hi)
        else:
            i = l // 2
            op, bp = odd_mixer(hp, l, None)
            cvp.append(bp)
            os_, bs = odd_mixer(hs, l, state_conv[i])
            cvs.append(bs)
        yp = yp + gp * op
        ys = ys + gs * os_

    return (yp, ys, jnp.stack(kp), jnp.stack(vp), jnp.stack(srp), jnp.stack(sip), jnp.stack(cvp),
            jnp.stack(kn), jnp.stack(vn), jnp.stack(srs), jnp.stack(sis), jnp.stack(cvs))
```

```python
import functools
import math

import jax
import jax.numpy as jnp
from jax import lax
from jax.experimental import pallas as pl
from jax.experimental.pallas import tpu as pltpu

F32 = jnp.float32
BF16 = jnp.bfloat16

CHUNK = 64
HEAD_DIM = 128
QK_DIM = 64
S5_GROUP = 16
S5_STATE = 64
CONV_W = 31
EPS = 1e-6
NEG_INF = -1e30

LANES = 128
SUBLANES = 8
S5_LANE_GROUPS = LANES // S5_GROUP
S5_TILE_STATES = S5_LANE_GROUPS * S5_STATE
VMEM_LIMIT = 48 * 1024 * 1024


def _params(*sem):
    return pltpu.CompilerParams(dimension_semantics=sem, vmem_limit_bytes=VMEM_LIMIT)


def _silu(x):
    return x * jax.nn.sigmoid(x)


def _ada_kernel(c_ref, w_ref, b_ref, o_ref):
    a = _silu(c_ref[...]).astype(BF16)
    w = w_ref[0].astype(BF16)
    o_ref[0] = jnp.dot(a, w, preferred_element_type=F32) + b_ref[0]


def _ada_mod(c_all, w_ada, b_ada):
    depth, d, n = w_ada.shape
    rows = c_all.shape[0]
    tn = 512
    return pl.pallas_call(
        _ada_kernel,
        grid=(depth, n // tn),
        in_specs=[pl.BlockSpec((rows, d), lambda l, j: (0, 0)),
                  pl.BlockSpec((1, d, tn), lambda l, j: (l, 0, j)),
                  pl.BlockSpec((1, 1, tn), lambda l, j: (l, 0, j))],
        out_specs=pl.BlockSpec((1, rows, tn), lambda l, j: (l, 0, j)),
        out_shape=jax.ShapeDtypeStruct((depth, rows, n), F32),
        compiler_params=_params("arbitrary", "arbitrary"),
        name="ada_mod",
    )(c_all, w_ada, b_ada.reshape(depth, 1, n))


def _modnorm_kernel(x_ref, sh_ref, sc_ref, g_ref, o_ref):
    x = x_ref[...]
    ms = jnp.mean(x * x, axis=-1, keepdims=True)
    y = x * lax.rsqrt(ms + EPS) * g_ref[...]
    o_ref[...] = (y * (1.0 + sc_ref[...]) + sh_ref[...]).astype(BF16)


def _row_spec(arr, tm, width, col_block):
    if arr.shape[0] == 1:
        return pl.BlockSpec((1, width), lambda i, c=col_block: (0, c))
    return pl.BlockSpec((tm, width), lambda i, c=col_block: (i, c))


def _modnorm(x, mod, g):
    m, d = x.shape
    tm = min(512, m)
    return pl.pallas_call(
        _modnorm_kernel,
        grid=(m // tm,),
        in_specs=[pl.BlockSpec((tm, d), lambda i: (i, 0)),
                  _row_spec(mod, tm, d, 0),
                  _row_spec(mod, tm, d, 1),
                  pl.BlockSpec((1, d), lambda i: (0, 0))],
        out_specs=pl.BlockSpec((tm, d), lambda i: (i, 0)),
        out_shape=jax.ShapeDtypeStruct((m, d), BF16),
        compiler_params=_params("arbitrary"),
        name="modnorm",
    )(x, mod, mod, g.reshape(1, d))


def _linear(name, xs, w, col0, ncols, ncb, epilogue, extras, outs, tm=512):
    m = xs[0].shape[0]
    tm = min(tm, m)
    k_total = w.shape[0]
    nx, ne = len(xs), len(extras)

    def kernel(*refs):
        x_refs, w_ref = refs[:nx], refs[nx]
        e_refs, o_refs = refs[nx + 1:nx + 1 + ne], refs[nx + 1 + ne:]
        acc, k0 = None, 0
        for xr in x_refs:
            kp = xr.shape[1]
            part = jnp.dot(xr[...], w_ref[k0:k0 + kp, :], preferred_element_type=F32)
            acc = part if acc is None else acc + part
            k0 += kp
        epilogue(acc, e_refs, o_refs)

    in_specs = [pl.BlockSpec((tm, x.shape[1]), lambda n, i: (i, 0)) for x in xs]
    in_specs.append(pl.BlockSpec((k_total, ncols), lambda n, i: (0, col0 + n)))
    in_specs += [pl.BlockSpec(bs, im) for _, bs, im in extras]
    res = pl.pallas_call(
        kernel,
        grid=(ncb, m // tm),
        in_specs=in_specs,
        out_specs=[pl.BlockSpec(bs, im) for _, _, bs, im in outs],
        out_shape=[jax.ShapeDtypeStruct(s, dt) for s, dt, _, _ in outs],
        compiler_params=_params("arbitrary", "arbitrary"),
        name=name,
    )(*xs, w, *[a for a, _, _ in extras])
    return res


def _half_mean_matrix():
    r = lax.broadcasted_iota(jnp.int32, (HEAD_DIM, HEAD_DIM), 0) // QK_DIM
    c = lax.broadcasted_iota(jnp.int32, (HEAD_DIM, HEAD_DIM), 1) // QK_DIM
    return jnp.where(r == c, 1.0 / QK_DIM, 0.0).astype(BF16)


def _qk_norm(acc, g_row, s_mat):
    heads = acc.shape[1] // HEAD_DIM
    outs = []
    for h in range(heads):
        xh = acc[:, h * HEAD_DIM:(h + 1) * HEAD_DIM]
        t = xh * xh
        hi = t.astype(BF16)
        lo = (t - hi.astype(F32)).astype(BF16)
        ms = (jnp.dot(hi, s_mat, preferred_element_type=F32)
              + jnp.dot(lo, s_mat, preferred_element_type=F32))
        outs.append(xh * lax.rsqrt(ms + EPS))
    return jnp.concatenate(outs, axis=1) * g_row


def _even_in_proj(h, w_in, q_g, k_g):
    m = h.shape[0]
    wdt = w_in.shape[1] // 6
    heads = wdt // HEAD_DIM
    tm = min(512, m)
    s_mat = _half_mean_matrix()
    row = lambda n, i: (i, 0)
    fix = lambda n, i: (0, 0)
    f32_out = ((m, wdt), F32, (tm, wdt), row)
    bf_out = ((m, wdt), BF16, (tm, wdt), row)
    qg = jnp.tile(q_g, heads).reshape(1, wdt)
    kg = jnp.tile(k_g, heads).reshape(1, wdt)
    norm_extras = lambda g: [(g, (1, wdt), fix), (s_mat, (HEAD_DIM, HEAD_DIM), fix)]

    def ep_plain(acc, e, o):
        o[0][...] = acc

    def ep_silu(acc, e, o):
        o[0][...] = _silu(acc)

    def ep_q(acc, e, o):
        o[0][...] = (_qk_norm(acc, e[0][...], e[1][...]) * (QK_DIM ** -0.5)).astype(BF16)

    def ep_k(acc, e, o):
        y = _qk_norm(acc, e[0][...], e[1][...])
        o[0][...] = y
        o[1][...] = y.astype(BF16)

    def ep_v(acc, e, o):
        o[0][...] = acc
        o[1][...] = acc.astype(BF16)

    (u,) = _linear("in_even_u", [h], w_in, 0, wdt, 1, ep_plain, [], [f32_out])
    (zs,) = _linear("in_even_zs", [h], w_in, 1, wdt, 1, ep_silu, [], [f32_out])
    (q,) = _linear("in_even_q", [h], w_in, 2, wdt, 1, ep_q, norm_extras(qg), [bf_out])
    k, kb = _linear("in_even_k", [h], w_in, 3, wdt, 1, ep_k, norm_extras(kg), [f32_out, bf_out])
    v, vb = _linear("in_even_v", [h], w_in, 4, wdt, 1, ep_v, [], [f32_out, bf_out])
    (za,) = _linear("in_even_za", [h], w_in, 5, wdt, 1, ep_silu, [], [f32_out])
    return u, zs, q, k, kb, v, vb, za


def _glu_proj(gy, w_glu, zs_gate):
    m, wdt = gy.shape
    tm = min(512, m)

    def ep(acc, e, o):
        o[0][...] = (acc[:, :wdt] * jax.nn.sigmoid(acc[:, wdt:]) * e[0][...]).astype(BF16)

    (out,) = _linear("s5_glu", [gy], w_glu, 0, 2 * wdt, 1, ep,
                     [(zs_gate, (tm, wdt), lambda n, i: (i, 0))],
                     [((m, wdt), BF16, (tm, wdt), lambda n, i: (i, 0))])
    return out


def _out_proj(name, xs, w, x_res, mod):
    m, d = x_res.shape
    tm = min(512, m)
    tn = min(1024, d)
    ncb = d // tn
    gate_col = 2 * ncb
    if mod.shape[0] == 1:
        gate = (mod, (1, tn), lambda n, i: (0, gate_col + n))
    else:
        gate = (mod, (tm, tn), lambda n, i: (i, gate_col + n))

    def ep(acc, e, o):
        o[0][...] = e[0][...] + e[1][...] * acc

    (y,) = _linear(name, xs, w, 0, tn, ncb, ep,
                   [(x_res, (tm, tn), lambda n, i: (i, n)), gate],
                   [((m, d), F32, (tm, tn), lambda n, i: (i, n))])
    return y


def _s5_kernel(u_ref, h0r_ref, h0i_ref, wbr_ref, wbi_ref, wcr_ref, wci_ref, d_ref, ar_ref, ai_ref,
               gy_ref, hr_ref, hi_ref, xr_s, xi_s, sr_s, si_s, coef_s, *, lane_chunk):
    b, t = pl.program_id(0), pl.program_id(1)
    rows, ns = xr_s.shape
    nc = wbr_ref.shape[0]

    @pl.when((b == 0) & (t == 0))
    def _():
        row = lax.broadcasted_iota(jnp.int32, (SUBLANES, ns), 0)
        a1 = (jnp.broadcast_to(ar_ref[...], (SUBLANES, ns)), jnp.broadcast_to(ai_ref[...], (SUBLANES, ns)))
        cmul = lambda x, y: (x[0] * y[0] - x[1] * y[1], x[0] * y[1] + x[1] * y[0])
        a2 = cmul(a1, a1)
        a4 = cmul(a2, a2)
        pc = a1
        for k in range(1, SUBLANES):
            nxt = cmul(pc, a1)
            pc = (jnp.where(row >= k, nxt[0], pc[0]), jnp.where(row >= k, nxt[1], pc[1]))
        for idx, (p, dist) in enumerate(((a1, 1), (a2, 2), (a4, 4))):
            coef_s[2 * idx] = jnp.where(row >= dist, p[0], 0.0)
            coef_s[2 * idx + 1] = jnp.where(row >= dist, p[1], 0.0)
        coef_s[6] = pc[0]
        coef_s[7] = pc[1]

    @pl.when(t == 0)
    def _():
        sr_s[...] = h0r_ref[0]
        si_s[...] = h0i_ref[0]

    ub = u_ref[...].astype(BF16)
    for c in range(nc):
        uc = ub[:, c * LANES:(c + 1) * LANES]
        cs = slice(c * S5_TILE_STATES, (c + 1) * S5_TILE_STATES)
        xr_s[:, cs] = jnp.dot(uc, wbr_ref[c], preferred_element_type=F32)
        xi_s[:, cs] = jnp.dot(uc, wbi_ref[c], preferred_element_type=F32)

    for lc in range(ns // lane_chunk):
        ls = slice(lc * lane_chunk, (lc + 1) * lane_chunk)
        m1r, m1i, m2r, m2i, m4r, m4i, pcr, pci = [coef_s[k, :, ls] for k in range(8)]

        def tile(ti, carry, ls=ls, m1r=m1r, m1i=m1i, m2r=m2r, m2i=m2i, m4r=m4r, m4i=m4i,
                 pcr=pcr, pci=pci):
            cr, ci = carry
            r0 = pl.multiple_of(ti * SUBLANES, SUBLANES)
            a = xr_s[pl.ds(r0, SUBLANES), ls]
            bb = xi_s[pl.ds(r0, SUBLANES), ls]
            for dist, mr, mi in ((1, m1r, m1i), (2, m2r, m2i), (4, m4r, m4i)):
                sa = pltpu.roll(a, dist, 0)
                sb = pltpu.roll(bb, dist, 0)
                a, bb = a + (mr * sa - mi * sb), bb + (mr * sb + mi * sa)
            a, bb = a + (pcr * cr - pci * ci), bb + (pcr * ci + pci * cr)
            xr_s[pl.ds(r0, SUBLANES), ls] = a
            xi_s[pl.ds(r0, SUBLANES), ls] = bb
            return a[SUBLANES - 1:SUBLANES, :], bb[SUBLANES - 1:SUBLANES, :]

        cr, ci = lax.fori_loop(0, rows // SUBLANES, tile, (sr_s[:, ls], si_s[:, ls]))
        sr_s[:, ls] = cr
        si_s[:, ls] = ci

    for c in range(nc):
        cs = slice(c * S5_TILE_STATES, (c + 1) * S5_TILE_STATES)
        os_ = slice(c * LANES, (c + 1) * LANES)
        y = (jnp.dot(xr_s[:, cs].astype(BF16), wcr_ref[c], preferred_element_type=F32)
             - jnp.dot(xi_s[:, cs].astype(BF16), wci_ref[c], preferred_element_type=F32)
             + d_ref[:, os_] * u_ref[:, os_])
        gy_ref[:, os_] = jax.nn.gelu(y).astype(BF16)

    @pl.when(t == pl.num_programs(1) - 1)
    def _():
        hr_ref[0] = sr_s[...]
        hi_ref[0] = si_s[...]


def _s5(u, h0r, h0i, prm, nb, lb):
    wdt = u.shape[1]
    ns = h0r.shape[-1]
    nc = wdt // LANES
    rows = min(256, lb)
    nt = lb // rows
    full = lambda a: pl.BlockSpec(a.shape, lambda b, t, nd=a.ndim: (0,) * nd)
    st = pl.BlockSpec((1, 1, ns), lambda b, t: (b, 0, 0))
    tok = pl.BlockSpec((rows, wdt), lambda b, t: (b * nt + t, 0))
    return pl.pallas_call(
        functools.partial(_s5_kernel, lane_chunk=min(512, ns)),
        grid=(nb, nt),
        in_specs=[tok, st, st] + [full(prm[k]) for k in ("wbr", "wbi", "wcr", "wci", "d", "ar", "ai")],
        out_specs=[tok, st, st],
        out_shape=[jax.ShapeDtypeStruct((nb * lb, wdt), BF16),
                   jax.ShapeDtypeStruct((nb, 1, ns), F32),
                   jax.ShapeDtypeStruct((nb, 1, ns), F32)],
        scratch_shapes=[pltpu.VMEM((rows, ns), F32), pltpu.VMEM((rows, ns), F32),
                        pltpu.VMEM((1, ns), F32), pltpu.VMEM((1, ns), F32),
                        pltpu.VMEM((8, SUBLANES, ns), F32)],
        compiler_params=_params("arbitrary", "arbitrary"),
        name="s5_scan",
    )(u, h0r, h0i, prm["wbr"], prm["wbi"], prm["wcr"], prm["wci"], prm["d"], prm["ar"], prm["ai"])


def _s5_params(lam_re, lam_im, log_dt, b_re, b_im, c_re, c_im, d_skip):
    g, p = lam_re.shape
    nc = g // S5_LANE_GROUPS
    dt = jnp.exp(log_dt)[:, None]
    mag = jnp.exp(lam_re * dt)
    ar = mag * jnp.cos(lam_im * dt)
    ai = mag * jnp.sin(lam_im * dt)
    den = lam_re * lam_re + lam_im * lam_im
    zr = ((ar - 1.0) * lam_re + ai * lam_im) / den
    zi = (ai * lam_re - (ar - 1.0) * lam_im) / den
    bbr = zr[..., None] * b_re - zi[..., None] * b_im
    bbi = zr[..., None] * b_im + zi[..., None] * b_re
    eye = jnp.eye(S5_LANE_GROUPS, dtype=F32)

    def bdiag_in(x):
        t = x.reshape(nc, S5_LANE_GROUPS, p, S5_GROUP).transpose(0, 1, 3, 2)
        return (t[:, :, :, None, :] * eye[None, :, None, :, None]).reshape(nc, LANES, S5_TILE_STATES).astype(BF16)

    def bdiag_out(x):
        t = x.reshape(nc, S5_LANE_GROUPS, S5_GROUP, p).transpose(0, 1, 3, 2)
        return (t[:, :, :, None, :] * eye[None, :, None, :, None]).reshape(nc, S5_TILE_STATES, LANES).astype(BF16)

    return dict(wbr=bdiag_in(bbr), wbi=bdiag_in(bbi), wcr=bdiag_out(c_re), wci=bdiag_out(c_im),
                d=d_skip.reshape(1, g * S5_GROUP), ar=ar.reshape(1, g * p), ai=ai.reshape(1, g * p))


def _stack_maps(q):
    lane = lax.broadcasted_iota(jnp.int32, q.shape, 1)
    zero = jnp.zeros_like(q)
    return jnp.concatenate([jnp.where(lane < QK_DIM, q, zero), jnp.where(lane >= QK_DIM, q, zero)], axis=0)


def _qk(qq, kb):
    return lax.dot_general(qq, kb, (((1,), (1,)), ((), ())), preferred_element_type=F32)


def _attn_finish(o, g_row, gate, lam_init):
    ms = jnp.mean(o * o, axis=-1, keepdims=True)
    return (o * lax.rsqrt(ms + EPS) * g_row * (1.0 - lam_init) * gate).astype(BF16)


def _attn_kernel(slopes_ref, lam_ref, q_ref, k_ref, v_ref, za_ref, og_ref, o_ref, m_s, l_s, acc_s,
                 *, lam_init):
    h, i = pl.program_id(0), pl.program_id(1)
    t = q_ref.shape[0]
    slope = slopes_ref[h]
    qq = _stack_maps(q_ref[...])
    m_s[...] = jnp.full(m_s.shape, -jnp.inf, F32)
    l_s[...] = jnp.zeros(l_s.shape, F32)
    acc_s[...] = jnp.zeros(acc_s.shape, F32)

    def update(s, vb):
        m_old = m_s[...]
        m_new = jnp.maximum(m_old, jnp.max(s, axis=1, keepdims=True))
        alpha = jnp.exp(m_old - m_new)
        p = jnp.exp(s - m_new)
        l_s[...] = alpha * l_s[...] + jnp.sum(p, axis=1, keepdims=True)
        acc_s[...] = alpha * acc_s[...] + jnp.dot(p.astype(BF16), vb, preferred_element_type=F32)
        m_s[...] = m_new

    k_local = lax.broadcasted_iota(jnp.int32, (1, t), 1).astype(F32)

    def off_diag(j, carry):
        r0 = pl.multiple_of(j * t, t)
        kb = k_ref[pl.ds(r0, t), :]
        vb = v_ref[pl.ds(r0, t), :]
        col_bias = slope * (k_local + ((j - i) * t).astype(F32))
        update(_qk(qq, kb) + col_bias, vb)
        return carry

    lax.fori_loop(0, i, off_diag, 0)

    r0 = pl.multiple_of(i * t, t)
    kb = k_ref[pl.ds(r0, t), :]
    vb = v_ref[pl.ds(r0, t), :]
    ql = lax.broadcasted_iota(jnp.int32, (2 * t, t), 0)
    ql = jnp.where(ql >= t, ql - t, ql)
    kl = lax.broadcasted_iota(jnp.int32, (2 * t, t), 1)
    vis = (kl // CHUNK) <= (ql // CHUNK)
    dist = jnp.abs(ql - kl).astype(F32)
    bias = jnp.where(vis, -slope * dist, NEG_INF) + slope * ql.astype(F32)
    update(_qk(qq, kb) + bias, vb)

    acc = acc_s[...]
    l = l_s[...]
    o = acc[:t] / l[:t] - lam_ref[0] * (acc[t:] / l[t:])
    o_ref[...] = _attn_finish(o, og_ref[...], za_ref[...], lam_init)


def _attention_prompt(q, kb, vb, za_gate, out_g, lam, lam_init):
    m, wdt = q.shape
    heads = wdt // HEAD_DIM
    t = min(512, m)
    slopes = 2.0 ** (-8.0 * jnp.arange(1, heads + 1, dtype=F32) / heads)
    smem = pl.BlockSpec(memory_space=pltpu.SMEM)
    blk = pl.BlockSpec((t, HEAD_DIM), lambda h, i: (i, h))
    head_all = pl.BlockSpec((m, HEAD_DIM), lambda h, i: (0, h))
    return pl.pallas_call(
        functools.partial(_attn_kernel, lam_init=lam_init),
        grid=(heads, m // t),
        in_specs=[smem, smem, blk, head_all, head_all, blk,
                  pl.BlockSpec((1, HEAD_DIM), lambda h, i: (0, 0))],
        out_specs=blk,
        out_shape=jax.ShapeDtypeStruct((m, wdt), BF16),
        scratch_shapes=[pltpu.VMEM((2 * t, 1), F32), pltpu.VMEM((2 * t, 1), F32),
                        pltpu.VMEM((2 * t, HEAD_DIM), F32)],
        compiler_params=_params("arbitrary", "arbitrary"),
        name="attn_prompt",
    )(slopes, lam, q, kb, vb, za_gate, out_g.reshape(1, HEAD_DIM))


def _attn_cached_kernel(lam_ref, q_ref, kn_ref, vn_ref, kc_ref, vc_ref, za_ref, og_ref, o_ref,
                        *, lam_init, heads):
    ls = q_ref.shape[0]
    past = kc_ref.shape[1]
    lam = lam_ref[0]
    ql = lax.broadcasted_iota(jnp.int32, (2 * ls, 1), 0)
    q_pos = past + jnp.where(ql >= ls, ql - ls, ql)
    kc_pos = lax.broadcasted_iota(jnp.int32, (1, past), 1)
    kn_pos = past + lax.broadcasted_iota(jnp.int32, (1, ls), 1)

    def bias(k_pos, slope):
        vis = (k_pos // CHUNK) <= (q_pos // CHUNK)
        return jnp.where(vis, -slope * jnp.abs(q_pos - k_pos).astype(F32), NEG_INF)

    for h in range(heads):
        hs = slice(h * HEAD_DIM, (h + 1) * HEAD_DIM)
        slope = 2.0 ** (-8.0 * (h + 1) / heads)
        qq = _stack_maps(q_ref[:, hs])
        sc = _qk(qq, kc_ref[0, :, hs].astype(BF16)) + bias(kc_pos, slope)
        sn = _qk(qq, kn_ref[:, hs].astype(BF16)) + bias(kn_pos, slope)
        mx = jnp.maximum(jnp.max(sc, axis=1, keepdims=True), jnp.max(sn, axis=1, keepdims=True))
        pc = jnp.exp(sc - mx)
        pn = jnp.exp(sn - mx)
        inv = 1.0 / (jnp.sum(pc, axis=1, keepdims=True) + jnp.sum(pn, axis=1, keepdims=True))
        pc = pc * inv
        pn = pn * inv
        wc = (pc[:ls] - lam * pc[ls:]).astype(BF16)
        wn = (pn[:ls] - lam * pn[ls:]).astype(BF16)
        o = (jnp.dot(wc, vc_ref[0, :, hs].astype(BF16), preferred_element_type=F32)
             + jnp.dot(wn, vn_ref[:, hs].astype(BF16), preferred_element_type=F32))
        o_ref[:, hs] = _attn_finish(o, og_ref[...], za_ref[:, hs], lam_init)


def _attention_cached(q, k_new, v_new, cache_k, cache_v, za_gate, out_g, lam, lam_init, nb, ls):
    wdt = q.shape[1]
    heads = wdt // HEAD_DIM
    past = cache_k.shape[1]
    tok = pl.BlockSpec((ls, wdt), lambda b: (b, 0))
    cache = pl.BlockSpec((1, past, wdt), lambda b: (b, 0, 0))
    return pl.pallas_call(
        functools.partial(_attn_cached_kernel, lam_init=lam_init, heads=heads),
        grid=(nb,),
        in_specs=[pl.BlockSpec(memory_space=pltpu.SMEM), tok, tok, tok, cache, cache, tok,
                  pl.BlockSpec((1, HEAD_DIM), lambda b: (0, 0))],
        out_specs=tok,
        out_shape=jax.ShapeDtypeStruct((nb * ls, wdt), BF16),
        compiler_params=_params("arbitrary"),
        name="attn_cached",
    )(lam, q, k_new, v_new, cache_k, cache_v, za_gate, out_g.reshape(1, HEAD_DIM))


def _odd_in_proj(h, w_in):
    m, d = h.shape
    tm = min(512, m)
    tn = min(512, d)
    ncb = d // tn

    def kernel(h_ref, wa_ref, wb_ref, wz_ref, g_ref, z_ref):
        x = h_ref[...]
        a = jnp.dot(x, wa_ref[...], preferred_element_type=F32)
        b = jnp.dot(x, wb_ref[...], preferred_element_type=F32)
        z = jnp.dot(x, wz_ref[...], preferred_element_type=F32)
        g_ref[...] = a * jax.nn.sigmoid(b)
        z_ref[...] = _silu(z)

    wspec = lambda off: pl.BlockSpec((d, tn), lambda n, i, off=off: (0, n + off))
    out = pl.BlockSpec((tm, tn), lambda n, i: (i, n))
    return pl.pallas_call(
        kernel,
        grid=(ncb, m // tm),
        in_specs=[pl.BlockSpec((tm, d), lambda n, i: (i, 0)), wspec(0), wspec(ncb), wspec(2 * ncb)],
        out_specs=[out, out],
        out_shape=[jax.ShapeDtypeStruct((m, d), F32), jax.ShapeDtypeStruct((m, d), F32)],
        compiler_params=_params("arbitrary", "arbitrary"),
        name="in_odd",
    )(h, w_in, w_in, w_in)


CONV_HALO = 32


def _conv_kernel(cur_ref, prev_ref, ctx_ref, zg_ref, w_ref, b_ref, lg_ref, lb_ref, o_ref, xp_s, y_s,
                 *, row_chunk):
    t = pl.program_id(1)
    tm, c = cur_ref.shape

    @pl.when(t == 0)
    def _():
        xp_s[0:CONV_HALO, :] = ctx_ref[0]

    @pl.when(t > 0)
    def _():
        xp_s[0:CONV_HALO, :] = prev_ref[...]

    xp_s[CONV_HALO:CONV_HALO + tm, :] = cur_ref[...]
    xp_s[CONV_HALO + tm:, :] = jnp.zeros((SUBLANES, c), F32)

    first = CONV_HALO - (CONV_W - 1)
    span = row_chunk + CONV_HALO

    def lane_body(lc, carry):
        c0 = pl.multiple_of(lc * LANES, LANES)

        def row_body(rc, carry2):
            r0 = pl.multiple_of(rc * row_chunk, row_chunk)
            blk = xp_s[pl.ds(r0, span + SUBLANES), pl.ds(c0, LANES)]
            acc = jnp.zeros((row_chunk, LANES), F32)
            for phase in range(SUBLANES):
                sh = blk[phase:phase + span]
                for a in range(span // SUBLANES):
                    tap = SUBLANES * a + phase - first
                    if 0 <= tap < CONV_W:
                        acc = acc + w_ref[pl.ds(tap, 1), pl.ds(c0, LANES)] * sh[SUBLANES * a:SUBLANES * a + row_chunk]
            y_s[pl.ds(r0, row_chunk), pl.ds(c0, LANES)] = acc
            return carry2

        return lax.fori_loop(0, tm // row_chunk, row_body, carry)

    lax.fori_loop(0, c // LANES, lane_body, 0)

    y = y_s[...] + b_ref[...]
    mu = jnp.mean(y, axis=-1, keepdims=True)
    yc = y - mu
    var = jnp.mean(yc * yc, axis=-1, keepdims=True)
    yn = yc * lax.rsqrt(var + EPS) * lg_ref[...] + lb_ref[...]
    o_ref[...] = (_silu(yn) * zg_ref[...]).astype(BF16)


def _conv_module(g, ctx, zg, conv_w, conv_b, ln_g, ln_b, nb, lb):
    c = g.shape[1]
    tm = min(256, lb)
    nt = lb // tm
    halo_blocks = tm // CONV_HALO
    w_pad = jnp.zeros((CONV_HALO, c), F32).at[:CONV_W].set(conv_w)
    cur = pl.BlockSpec((tm, c), lambda b, t: (b * nt + t, 0))
    prev = pl.BlockSpec((CONV_HALO, c), lambda b, t: (jnp.maximum((b * nt + t) * halo_blocks - 1, 0), 0))
    vec = pl.BlockSpec((1, c), lambda b, t: (0, 0))
    return pl.pallas_call(
        functools.partial(_conv_kernel, row_chunk=min(64, tm)),
        grid=(nb, nt),
        in_specs=[cur, prev, pl.BlockSpec((1, CONV_HALO, c), lambda b, t: (b, 0, 0)), cur,
                  pl.BlockSpec((CONV_HALO, c), lambda b, t: (0, 0)), vec, vec, vec],
        out_specs=cur,
        out_shape=jax.ShapeDtypeStruct((nb * lb, c), BF16),
        scratch_shapes=[pltpu.VMEM((CONV_HALO + tm + SUBLANES, c), F32), pltpu.VMEM((tm, c), F32)],
        compiler_params=_params("arbitrary", "arbitrary"),
        name="conv_module",
    )(g, g, ctx, zg, w_pad, conv_b.reshape(1, c), ln_g.reshape(1, c), ln_b.reshape(1, c))


def kernel(x_prompt, x_sample, c_prompt, c_sample, cache_k, cache_v, state_s5_re, state_s5_im, state_conv, norm_g, w_ada, b_ada, w_in_even, w_out_even, s5_lam_re, s5_lam_im, s5_log_dt, s5_b_re, s5_b_im, s5_c_re, s5_c_im, s5_d, s5_w_glu, q_norm_g, k_norm_g, lam_q1, lam_k1, lam_q2, lam_k2, attn_out_g, w_in_odd, conv_w, conv_b, conv_ln_g, conv_ln_b, w_out_odd):
    bp, lp, d = x_prompt.shape
    bs, ls, _ = x_sample.shape
    assert bp == 1 and w_ada.shape[0] == 2, "one prompt sequence, one even and one odd layer"
    wdt = d // 2
    heads = wdt // HEAD_DIM
    ns = (wdt // S5_GROUP) * S5_STATE
    ms = bs * ls

    xp = x_prompt.reshape(lp, d)
    xs = x_sample.reshape(ms, d)

    rows = bp + bs
    rows_pad = -(-rows // 16) * 16
    c_all = jnp.concatenate([c_prompt, c_sample, jnp.zeros((rows_pad - rows, d), F32)], axis=0)
    mod = _ada_mod(c_all, w_ada, b_ada)
    mod_p = [mod[l, :bp] for l in range(2)]
    mod_s = [jnp.repeat(mod[l, bp:rows], ls, axis=0) for l in range(2)]

    w_in_e = w_in_even[0].astype(BF16)
    w_out_e = w_out_even[0].astype(BF16)
    w_glu = s5_w_glu[0].astype(BF16)
    w_in_o = w_in_odd[0].astype(BF16)
    w_out_o = w_out_odd[0].astype(BF16)

    s5p = _s5_params(s5_lam_re[0], s5_lam_im[0], s5_log_dt[0], s5_b_re[0], s5_b_im[0],
                     s5_c_re[0], s5_c_im[0], s5_d[0])
    lam_init = 0.8 - 0.6 * math.exp(-0.3 * 0)
    lam = (jnp.exp(jnp.sum(lam_q1[0] * lam_k1[0])) - jnp.exp(jnp.sum(lam_q2[0] * lam_k2[0]))
           + lam_init).reshape(1).astype(F32)

    def even_layer(x, modl, nb, lb, h0r, h0i, cache):
        h = _modnorm(x, modl, norm_g[0])
        u, zs, q, k, kb, v, vb, za = _even_in_proj(h, w_in_e, q_norm_g[0], k_norm_g[0])
        gy, hr, hi = _s5(u, h0r, h0i, s5p, nb, lb)
        s5_out = _glu_proj(gy, w_glu, zs)
        if cache is None:
            att = _attention_prompt(q, kb, vb, za, attn_out_g[0], lam, lam_init)
        else:
            att = _attention_cached(q, k, v, cache[0], cache[1], za, attn_out_g[0], lam, lam_init, nb, lb)
        y = _out_proj("out_even", [s5_out, att], w_out_e, x, modl)
        return y, k, v, hr, hi

    def odd_layer(x, modl, nb, lb, ctx):
        h = _modnorm(x, modl, norm_g[1])
        g, zg = _odd_in_proj(h, w_in_o)
        yc = _conv_module(g, ctx, zg, conv_w[0], conv_b[0], conv_ln_g[0], conv_ln_b[0], nb, lb)
        y = _out_proj("out_odd", [yc], w_out_o, x, modl)
        return y, g

    zero_state = jnp.zeros((bp, 1, ns), F32)
    yp, kp, vp, hrp, hip = even_layer(xp, mod_p[0], bp, lp, zero_state, zero_state, None)
    past = cache_k.shape[2]
    ys, ksn, vsn, hrs, his = even_layer(
        xs, mod_s[0], bs, ls, state_s5_re[0].reshape(bs, 1, ns), state_s5_im[0].reshape(bs, 1, ns),
        (cache_k[0].reshape(bs, past, wdt), cache_v[0].reshape(bs, past, wdt)))

    pad = CONV_HALO - (CONV_W - 1)
    ctx_p = jnp.zeros((bp, CONV_HALO, d), F32)
    ctx_s = jnp.pad(state_conv[0], ((0, 0), (pad, 0), (0, 0)))
    yp, gp = odd_layer(yp, mod_p[1], bp, lp, ctx_p)
    ys, gs = odd_layer(ys, mod_s[1], bs, ls, ctx_s)

    groups = wdt // S5_GROUP
    tail = CONV_W - 1
    conv_p = gp.reshape(bp, lp, d)[:, lp - tail:]
    conv_s = jnp.concatenate([state_conv[0], gs.reshape(bs, ls, d)], axis=1)[:, -tail:]
    return (yp.reshape(bp, lp, d), ys.reshape(bs, ls, d),
            kp.reshape(1, bp, lp, heads, HEAD_DIM), vp.reshape(1, bp, lp, heads, HEAD_DIM),
            hrp.reshape(1, bp, groups, S5_STATE), hip.reshape(1, bp, groups, S5_STATE), conv_p[None],
            ksn.reshape(1, bs, ls, heads, HEAD_DIM), vsn.reshape(1, bs, ls, heads, HEAD_DIM),
            hrs.reshape(1, bs, groups, S5_STATE), his.reshape(1, bs, groups, S5_STATE), conv_s[None])
```

```python
import functools
import math

import jax
import jax.numpy as jnp
from jax import lax
from jax.experimental import pallas as pl
from jax.experimental.pallas import tpu as pltpu

F32 = jnp.float32
BF16 = jnp.bfloat16

CHUNK = 64
HEAD_DIM = 128
QK_DIM = 64
S5_GROUP = 16
S5_STATE = 64
CONV_W = 31
EPS = 1e-6
NEG_INF = -1e30
LOG2E = 1.4426950408889634

LANES = 128
SUBLANES = 8
S5_LANE_GROUPS = LANES // S5_GROUP
S5_TILE_STATES = S5_LANE_GROUPS * S5_STATE
VMEM_LIMIT = 48 * 1024 * 1024


def _params(*sem):
    return pltpu.CompilerParams(dimension_semantics=sem, vmem_limit_bytes=VMEM_LIMIT)


def _silu(x):
    return x * jax.nn.sigmoid(x)


def _ada_kernel(c_ref, w_ref, b_ref, o_ref):
    a = _silu(c_ref[...]).astype(BF16)
    w = w_ref[0].astype(BF16)
    o_ref[0] = jnp.dot(a, w, preferred_element_type=F32) + b_ref[0]


def _ada_mod(c_all, w_ada, b_ada):
    depth, d, n = w_ada.shape
    rows = c_all.shape[0]
    tn = 512
    return pl.pallas_call(
        _ada_kernel,
        grid=(depth, n // tn),
        in_specs=[pl.BlockSpec((rows, d), lambda l, j: (0, 0)),
                  pl.BlockSpec((1, d, tn), lambda l, j: (l, 0, j)),
                  pl.BlockSpec((1, 1, tn), lambda l, j: (l, 0, j))],
        out_specs=pl.BlockSpec((1, rows, tn), lambda l, j: (l, 0, j)),
        out_shape=jax.ShapeDtypeStruct((depth, rows, n), F32),
        compiler_params=_params("arbitrary", "arbitrary"),
        name="ada_mod",
    )(c_all, w_ada, b_ada.reshape(depth, 1, n))


def _modnorm_kernel(x_ref, sh_ref, sc_ref, g_ref, o_ref):
    x = x_ref[...]
    ms = jnp.mean(x * x, axis=-1, keepdims=True)
    y = x * lax.rsqrt(ms + EPS) * g_ref[...]
    o_ref[...] = (y * (1.0 + sc_ref[...]) + sh_ref[...]).astype(BF16)


def _row_spec(arr, tm, width, col_block):
    if arr.shape[0] == 1:
        return pl.BlockSpec((1, width), lambda i, c=col_block: (0, c))
    return pl.BlockSpec((tm, width), lambda i, c=col_block: (i, c))


def _modnorm(x, mod, g):
    m, d = x.shape
    tm = min(512, m)
    return pl.pallas_call(
        _modnorm_kernel,
        grid=(m // tm,),
        in_specs=[pl.BlockSpec((tm, d), lambda i: (i, 0)),
                  _row_spec(mod, tm, d, 0),
                  _row_spec(mod, tm, d, 1),
                  pl.BlockSpec((1, d), lambda i: (0, 0))],
        out_specs=pl.BlockSpec((tm, d), lambda i: (i, 0)),
        out_shape=jax.ShapeDtypeStruct((m, d), BF16),
        compiler_params=_params("arbitrary"),
        name="modnorm",
    )(x, mod, mod, g.reshape(1, d))


def _linear(name, xs, w, col0, ncols, ncb, epilogue, extras, outs, tm=512):
    m = xs[0].shape[0]
    tm = min(tm, m)
    k_total = w.shape[0]
    nx, ne = len(xs), len(extras)

    def kernel(*refs):
        x_refs, w_ref = refs[:nx], refs[nx]
        e_refs, o_refs = refs[nx + 1:nx + 1 + ne], refs[nx + 1 + ne:]
        acc, k0 = None, 0
        for xr in x_refs:
            kp = xr.shape[1]
            part = jnp.dot(xr[...], w_ref[k0:k0 + kp, :], preferred_element_type=F32)
            acc = part if acc is None else acc + part
            k0 += kp
        epilogue(acc, e_refs, o_refs)

    in_specs = [pl.BlockSpec((tm, x.shape[1]), lambda n, i: (i, 0)) for x in xs]
    in_specs.append(pl.BlockSpec((k_total, ncols), lambda n, i: (0, col0 + n)))
    in_specs += [pl.BlockSpec(bs, im) for _, bs, im in extras]
    res = pl.pallas_call(
        kernel,
        grid=(ncb, m // tm),
        in_specs=in_specs,
        out_specs=[pl.BlockSpec(bs, im) for _, _, bs, im in outs],
        out_shape=[jax.ShapeDtypeStruct(s, dt) for s, dt, _, _ in outs],
        compiler_params=_params("arbitrary", "arbitrary"),
        name=name,
    )(*xs, w, *[a for a, _, _ in extras])
    return res


def _half_mean_matrix():
    r = lax.broadcasted_iota(jnp.int32, (HEAD_DIM, HEAD_DIM), 0) // QK_DIM
    c = lax.broadcasted_iota(jnp.int32, (HEAD_DIM, HEAD_DIM), 1) // QK_DIM
    return jnp.where(r == c, 1.0 / QK_DIM, 0.0).astype(BF16)


def _qk_norm(acc, g_row, s_mat):
    heads = acc.shape[1] // HEAD_DIM
    outs = []
    for h in range(heads):
        xh = acc[:, h * HEAD_DIM:(h + 1) * HEAD_DIM]
        t = xh * xh
        hi = t.astype(BF16)
        lo = (t - hi.astype(F32)).astype(BF16)
        ms = (jnp.dot(hi, s_mat, preferred_element_type=F32)
              + jnp.dot(lo, s_mat, preferred_element_type=F32))
        outs.append(xh * lax.rsqrt(ms + EPS))
    return jnp.concatenate(outs, axis=1) * g_row


def _even_in_proj(h, w_in, q_g, k_g):
    m = h.shape[0]
    wdt = w_in.shape[1] // 6
    heads = wdt // HEAD_DIM
    tm = min(512, m)
    s_mat = _half_mean_matrix()
    row = lambda n, i: (i, 0)
    fix = lambda n, i: (0, 0)
    f32_out = ((m, wdt), F32, (tm, wdt), row)
    bf_out = ((m, wdt), BF16, (tm, wdt), row)
    qg = jnp.tile(q_g, heads).reshape(1, wdt)
    kg = jnp.tile(k_g, heads).reshape(1, wdt)
    norm_extras = lambda g: [(g, (1, wdt), fix), (s_mat, (HEAD_DIM, HEAD_DIM), fix)]

    def ep_plain(acc, e, o):
        o[0][...] = acc

    def ep_silu(acc, e, o):
        o[0][...] = _silu(acc)

    def ep_q(acc, e, o):
        o[0][...] = (_qk_norm(acc, e[0][...], e[1][...]) * (LOG2E * QK_DIM ** -0.5)).astype(BF16)

    def ep_k(acc, e, o):
        y = _qk_norm(acc, e[0][...], e[1][...])
        o[0][...] = y
        o[1][...] = y.astype(BF16)

    def ep_v(acc, e, o):
        o[0][...] = acc
        o[1][...] = acc.astype(BF16)

    (u,) = _linear("in_even_u", [h], w_in, 0, wdt, 1, ep_plain, [], [f32_out])
    (zs,) = _linear("in_even_zs", [h], w_in, 1, wdt, 1, ep_silu, [], [f32_out])
    (q,) = _linear("in_even_q", [h], w_in, 2, wdt, 1, ep_q, norm_extras(qg), [bf_out])
    k, kb = _linear("in_even_k", [h], w_in, 3, wdt, 1, ep_k, norm_extras(kg), [f32_out, bf_out])
    v, vb = _linear("in_even_v", [h], w_in, 4, wdt, 1, ep_v, [], [f32_out, bf_out])
    (za,) = _linear("in_even_za", [h], w_in, 5, wdt, 1, ep_silu, [], [f32_out])
    return u, zs, q, k, kb, v, vb, za


def _glu_proj(gy, w_glu, zs_gate):
    m, wdt = gy.shape
    tm = min(512, m)

    def ep(acc, e, o):
        o[0][...] = (acc[:, :wdt] * jax.nn.sigmoid(acc[:, wdt:]) * e[0][...]).astype(BF16)

    (out,) = _linear("s5_glu", [gy], w_glu, 0, 2 * wdt, 1, ep,
                     [(zs_gate, (tm, wdt), lambda n, i: (i, 0))],
                     [((m, wdt), BF16, (tm, wdt), lambda n, i: (i, 0))])
    return out


def _out_proj(name, xs, w, x_res, mod):
    m, d = x_res.shape
    tm = min(512, m)
    tn = min(1024, d)
    ncb = d // tn
    gate_col = 2 * ncb
    if mod.shape[0] == 1:
        gate = (mod, (1, tn), lambda n, i: (0, gate_col + n))
    else:
        gate = (mod, (tm, tn), lambda n, i: (i, gate_col + n))

    def ep(acc, e, o):
        o[0][...] = e[0][...] + e[1][...] * acc

    (y,) = _linear(name, xs, w, 0, tn, ncb, ep,
                   [(x_res, (tm, tn), lambda n, i: (i, n)), gate],
                   [((m, d), F32, (tm, tn), lambda n, i: (i, n))])
    return y


def _s5_kernel(u_ref, h0r_ref, h0i_ref, wbr_ref, wbi_ref, wcr_ref, wci_ref, d_ref, ar_ref, ai_ref,
               gy_ref, hr_ref, hi_ref, xr_s, xi_s, sr_s, si_s, coef_s, *, lane_chunk):
    b, t = pl.program_id(0), pl.program_id(1)
    rows, ns = xr_s.shape
    nc = wbr_ref.shape[0]

    @pl.when((b == 0) & (t == 0))
    def _():
        row = lax.broadcasted_iota(jnp.int32, (SUBLANES, ns), 0)
        a1 = (jnp.broadcast_to(ar_ref[...], (SUBLANES, ns)), jnp.broadcast_to(ai_ref[...], (SUBLANES, ns)))
        cmul = lambda x, y: (x[0] * y[0] - x[1] * y[1], x[0] * y[1] + x[1] * y[0])
        a2 = cmul(a1, a1)
        a4 = cmul(a2, a2)
        pc = a1
        for k in range(1, SUBLANES):
            nxt = cmul(pc, a1)
            pc = (jnp.where(row >= k, nxt[0], pc[0]), jnp.where(row >= k, nxt[1], pc[1]))
        for idx, (p, dist) in enumerate(((a1, 1), (a2, 2), (a4, 4))):
            coef_s[2 * idx] = jnp.where(row >= dist, p[0], 0.0)
            coef_s[2 * idx + 1] = jnp.where(row >= dist, p[1], 0.0)
        coef_s[6] = pc[0]
        coef_s[7] = pc[1]

    @pl.when(t == 0)
    def _():
        sr_s[...] = h0r_ref[0]
        si_s[...] = h0i_ref[0]

    ub = u_ref[...].astype(BF16)
    for c in range(nc):
        uc = ub[:, c * LANES:(c + 1) * LANES]
        cs = slice(c * S5_TILE_STATES, (c + 1) * S5_TILE_STATES)
        xr_s[:, cs] = jnp.dot(uc, wbr_ref[c], preferred_element_type=F32)
        xi_s[:, cs] = jnp.dot(uc, wbi_ref[c], preferred_element_type=F32)

    for lc in range(ns // lane_chunk):
        ls = slice(lc * lane_chunk, (lc + 1) * lane_chunk)
        m1r, m1i, m2r, m2i, m4r, m4i, pcr, pci = [coef_s[k, :, ls] for k in range(8)]

        def tile(ti, carry, ls=ls, m1r=m1r, m1i=m1i, m2r=m2r, m2i=m2i, m4r=m4r, m4i=m4i,
                 pcr=pcr, pci=pci):
            cr, ci = carry
            r0 = pl.multiple_of(ti * SUBLANES, SUBLANES)
            a = xr_s[pl.ds(r0, SUBLANES), ls]
            bb = xi_s[pl.ds(r0, SUBLANES), ls]
            for dist, mr, mi in ((1, m1r, m1i), (2, m2r, m2i), (4, m4r, m4i)):
                sa = pltpu.roll(a, dist, 0)
                sb = pltpu.roll(bb, dist, 0)
                a, bb = a + (mr * sa - mi * sb), bb + (mr * sb + mi * sa)
            a, bb = a + (pcr * cr - pci * ci), bb + (pcr * ci + pci * cr)
            xr_s[pl.ds(r0, SUBLANES), ls] = a
            xi_s[pl.ds(r0, SUBLANES), ls] = bb
            return a[SUBLANES - 1:SUBLANES, :], bb[SUBLANES - 1:SUBLANES, :]

        cr, ci = lax.fori_loop(0, rows // SUBLANES, tile, (sr_s[:, ls], si_s[:, ls]))
        sr_s[:, ls] = cr
        si_s[:, ls] = ci

    for c in range(nc):
        cs = slice(c * S5_TILE_STATES, (c + 1) * S5_TILE_STATES)
        os_ = slice(c * LANES, (c + 1) * LANES)
        y = (jnp.dot(xr_s[:, cs].astype(BF16), wcr_ref[c], preferred_element_type=F32)
             - jnp.dot(xi_s[:, cs].astype(BF16), wci_ref[c], preferred_element_type=F32)
             + d_ref[:, os_] * u_ref[:, os_])
        gy_ref[:, os_] = jax.nn.gelu(y).astype(BF16)

    @pl.when(t == pl.num_programs(1) - 1)
    def _():
        hr_ref[0] = sr_s[...]
        hi_ref[0] = si_s[...]


def _s5(u, h0r, h0i, prm, nb, lb):
    wdt = u.shape[1]
    ns = h0r.shape[-1]
    nc = wdt // LANES
    rows = min(256, lb)
    nt = lb // rows
    full = lambda a: pl.BlockSpec(a.shape, lambda b, t, nd=a.ndim: (0,) * nd)
    st = pl.BlockSpec((1, 1, ns), lambda b, t: (b, 0, 0))
    tok = pl.BlockSpec((rows, wdt), lambda b, t: (b * nt + t, 0))
    return pl.pallas_call(
        functools.partial(_s5_kernel, lane_chunk=min(512, ns)),
        grid=(nb, nt),
        in_specs=[tok, st, st] + [full(prm[k]) for k in ("wbr", "wbi", "wcr", "wci", "d", "ar", "ai")],
        out_specs=[tok, st, st],
        out_shape=[jax.ShapeDtypeStruct((nb * lb, wdt), BF16),
                   jax.ShapeDtypeStruct((nb, 1, ns), F32),
                   jax.ShapeDtypeStruct((nb, 1, ns), F32)],
        scratch_shapes=[pltpu.VMEM((rows, ns), F32), pltpu.VMEM((rows, ns), F32),
                        pltpu.VMEM((1, ns), F32), pltpu.VMEM((1, ns), F32),
                        pltpu.VMEM((8, SUBLANES, ns), F32)],
        compiler_params=_params("arbitrary", "arbitrary"),
        name="s5_scan",
    )(u, h0r, h0i, prm["wbr"], prm["wbi"], prm["wcr"], prm["wci"], prm["d"], prm["ar"], prm["ai"])


def _s5_params(lam_re, lam_im, log_dt, b_re, b_im, c_re, c_im, d_skip):
    g, p = lam_re.shape
    nc = g // S5_LANE_GROUPS
    dt = jnp.exp(log_dt)[:, None]
    mag = jnp.exp(lam_re * dt)
    ar = mag * jnp.cos(lam_im * dt)
    ai = mag * jnp.sin(lam_im * dt)
    den = lam_re * lam_re + lam_im * lam_im
    zr = ((ar - 1.0) * lam_re + ai * lam_im) / den
    zi = (ai * lam_re - (ar - 1.0) * lam_im) / den
    bbr = zr[..., None] * b_re - zi[..., None] * b_im
    bbi = zr[..., None] * b_im + zi[..., None] * b_re
    eye = jnp.eye(S5_LANE_GROUPS, dtype=F32)

    def bdiag_in(x):
        t = x.reshape(nc, S5_LANE_GROUPS, p, S5_GROUP).transpose(0, 1, 3, 2)
        return (t[:, :, :, None, :] * eye[None, :, None, :, None]).reshape(nc, LANES, S5_TILE_STATES).astype(BF16)

    def bdiag_out(x):
        t = x.reshape(nc, S5_LANE_GROUPS, S5_GROUP, p).transpose(0, 1, 3, 2)
        return (t[:, :, :, None, :] * eye[None, :, None, :, None]).reshape(nc, S5_TILE_STATES, LANES).astype(BF16)

    return dict(wbr=bdiag_in(bbr), wbi=bdiag_in(bbi), wcr=bdiag_out(c_re), wci=bdiag_out(c_im),
                d=d_skip.reshape(1, g * S5_GROUP), ar=ar.reshape(1, g * p), ai=ai.reshape(1, g * p))


def _stack_maps(q):
    lane = lax.broadcasted_iota(jnp.int32, q.shape, 1)
    zero = jnp.zeros_like(q)
    return jnp.concatenate([jnp.where(lane < QK_DIM, q, zero), jnp.where(lane >= QK_DIM, q, zero)], axis=0)


def _qk(qq, kb):
    return lax.dot_general(qq, kb, (((1,), (1,)), ((), ())), preferred_element_type=F32)


def _attn_finish(o, g_row, gate, lam_init):
    ms = jnp.mean(o * o, axis=-1, keepdims=True)
    return (o * lax.rsqrt(ms + EPS) * g_row * (1.0 - lam_init) * gate).astype(BF16)


def _attn_kernel(slopes_ref, lam_ref, q_ref, k_ref, vt_ref, za_ref, og_ref, o_ref, m_s, l_s, acc_s,
                 *, lam_init):
    h, i = pl.program_id(0), pl.program_id(1)
    t = q_ref.shape[0]
    nq = 2 * t
    slope = slopes_ref[h]
    qq = _stack_maps(q_ref[...])
    m_s[...] = jnp.full(m_s.shape, -jnp.inf, F32)
    l_s[...] = jnp.zeros(l_s.shape, F32)
    acc_s[...] = jnp.zeros(acc_s.shape, F32)

    def update(s_t, shift, vtb):
        m_old = m_s[...]
        m_new = jnp.maximum(m_old, jnp.max(s_t, axis=0, keepdims=True) + shift)
        alpha = jnp.exp2(m_old - m_new)
        p = jnp.exp2(s_t - (m_new - shift))
        l_s[...] = alpha * l_s[...] + jnp.sum(p, axis=0, keepdims=True)
        acc_s[...] = alpha * acc_s[...] + jnp.dot(vtb, p.astype(BF16), preferred_element_type=F32)
        m_s[...] = m_new

    k_local = lax.broadcasted_iota(jnp.int32, (t, nq), 0)
    k_bias = slope * k_local.astype(F32)

    def off_diag(j, carry):
        r0 = pl.multiple_of(j * t, t)
        s_t = _qk(k_ref[pl.ds(r0, t), :], qq) + k_bias
        update(s_t, slope * ((j - i) * t).astype(F32), vt_ref[:, pl.ds(r0, t)])
        return carry

    lax.fori_loop(0, i, off_diag, 0)

    r0 = pl.multiple_of(i * t, t)
    q_col = lax.broadcasted_iota(jnp.int32, (t, nq), 1)
    q_local = jnp.where(q_col >= t, q_col - t, q_col)
    vis = (k_local // CHUNK) <= (q_local // CHUNK)
    dist = jnp.abs(q_local - k_local).astype(F32)
    bias = jnp.where(vis, -slope * dist, NEG_INF) + slope * q_local.astype(F32)
    update(_qk(k_ref[pl.ds(r0, t), :], qq) + bias, 0.0, vt_ref[:, pl.ds(r0, t)])

    acc = acc_s[...]
    l = l_s[...]
    o_t = acc[:, :t] / l[:, :t] - lam_ref[0] * (acc[:, t:] / l[:, t:])
    ms = jnp.mean(o_t * o_t, axis=0, keepdims=True)
    y = (o_t * lax.rsqrt(ms + EPS)).T
    o_ref[...] = (y * og_ref[...] * (1.0 - lam_init) * za_ref[...]).astype(BF16)


def _attention_prompt(q, kb, vb, za_gate, out_g, lam, lam_init):
    m, wdt = q.shape
    heads = wdt // HEAD_DIM
    t = min(512, m)
    slopes = LOG2E * 2.0 ** (-8.0 * jnp.arange(1, heads + 1, dtype=F32) / heads)
    smem = pl.BlockSpec(memory_space=pltpu.SMEM)
    blk = pl.BlockSpec((t, HEAD_DIM), lambda h, i: (i, h))
    return pl.pallas_call(
        functools.partial(_attn_kernel, lam_init=lam_init),
        grid=(heads, m // t),
        in_specs=[smem, smem, blk,
                  pl.BlockSpec((m, HEAD_DIM), lambda h, i: (0, h)),
                  pl.BlockSpec((HEAD_DIM, m), lambda h, i: (h, 0)),
                  blk, pl.BlockSpec((1, HEAD_DIM), lambda h, i: (0, 0))],
        out_specs=blk,
        out_shape=jax.ShapeDtypeStruct((m, wdt), BF16),
        scratch_shapes=[pltpu.VMEM((1, 2 * t), F32), pltpu.VMEM((1, 2 * t), F32),
                        pltpu.VMEM((HEAD_DIM, 2 * t), F32)],
        compiler_params=_params("arbitrary", "arbitrary"),
        name="attn_prompt",
    )(slopes, lam, q, kb, vb.T, za_gate, out_g.reshape(1, HEAD_DIM))


def _attn_cached_kernel(lam_ref, q_ref, kn_ref, vn_ref, kc_ref, vc_ref, za_ref, og_ref, o_ref,
                        *, lam_init, heads):
    ls = q_ref.shape[0]
    past = kc_ref.shape[1]
    lam = lam_ref[0]
    ql = lax.broadcasted_iota(jnp.int32, (2 * ls, 1), 0)
    q_pos = past + jnp.where(ql >= ls, ql - ls, ql)
    kc_pos = lax.broadcasted_iota(jnp.int32, (1, past), 1)
    kn_pos = past + lax.broadcasted_iota(jnp.int32, (1, ls), 1)

    def bias(k_pos, slope):
        vis = (k_pos // CHUNK) <= (q_pos // CHUNK)
        return jnp.where(vis, -slope * jnp.abs(q_pos - k_pos).astype(F32), NEG_INF)

    for h in range(heads):
        hs = slice(h * HEAD_DIM, (h + 1) * HEAD_DIM)
        slope = LOG2E * 2.0 ** (-8.0 * (h + 1) / heads)
        qq = _stack_maps(q_ref[:, hs])
        sc = _qk(qq, kc_ref[0, :, hs].astype(BF16)) + bias(kc_pos, slope)
        sn = _qk(qq, kn_ref[:, hs].astype(BF16)) + bias(kn_pos, slope)
        mx = jnp.maximum(jnp.max(sc, axis=1, keepdims=True), jnp.max(sn, axis=1, keepdims=True))
        pc = jnp.exp2(sc - mx)
        pn = jnp.exp2(sn - mx)
        inv = 1.0 / (jnp.sum(pc, axis=1, keepdims=True) + jnp.sum(pn, axis=1, keepdims=True))
        pc = pc * inv
        pn = pn * inv
        wc = (pc[:ls] - lam * pc[ls:]).astype(BF16)
        wn = (pn[:ls] - lam * pn[ls:]).astype(BF16)
        o = (jnp.dot(wc, vc_ref[0, :, hs].astype(BF16), preferred_element_type=F32)
             + jnp.dot(wn, vn_ref[:, hs].astype(BF16), preferred_element_type=F32))
        o_ref[:, hs] = _attn_finish(o, og_ref[...], za_ref[:, hs], lam_init)


def _attention_cached(q, k_new, v_new, cache_k, cache_v, za_gate, out_g, lam, lam_init, nb, ls):
    wdt = q.shape[1]
    heads = wdt // HEAD_DIM
    past = cache_k.shape[1]
    tok = pl.BlockSpec((ls, wdt), lambda b: (b, 0))
    cache = pl.BlockSpec((1, past, wdt), lambda b: (b, 0, 0))
    return pl.pallas_call(
        functools.partial(_attn_cached_kernel, lam_init=lam_init, heads=heads),
        grid=(nb,),
        in_specs=[pl.BlockSpec(memory_space=pltpu.SMEM), tok, tok, tok, cache, cache, tok,
                  pl.BlockSpec((1, HEAD_DIM), lambda b: (0, 0))],
        out_specs=tok,
        out_shape=jax.ShapeDtypeStruct((nb * ls, wdt), BF16),
        compiler_params=_params("arbitrary"),
        name="attn_cached",
    )(lam, q, k_new, v_new, cache_k, cache_v, za_gate, out_g.reshape(1, HEAD_DIM))


def _odd_in_proj(h, w_in):
    m, d = h.shape
    tm = min(512, m)
    tn = min(512, d)
    ncb = d // tn

    def kernel(h_ref, wa_ref, wb_ref, wz_ref, g_ref, z_ref):
        x = h_ref[...]
        a = jnp.dot(x, wa_ref[...], preferred_element_type=F32)
        b = jnp.dot(x, wb_ref[...], preferred_element_type=F32)
        z = jnp.dot(x, wz_ref[...], preferred_element_type=F32)
        g_ref[...] = a * jax.nn.sigmoid(b)
        z_ref[...] = _silu(z)

    wspec = lambda off: pl.BlockSpec((d, tn), lambda n, i, off=off: (0, n + off))
    out = pl.BlockSpec((tm, tn), lambda n, i: (i, n))
    return pl.pallas_call(
        kernel,
        grid=(ncb, m // tm),
        in_specs=[pl.BlockSpec((tm, d), lambda n, i: (i, 0)), wspec(0), wspec(ncb), wspec(2 * ncb)],
        out_specs=[out, out],
        out_shape=[jax.ShapeDtypeStruct((m, d), F32), jax.ShapeDtypeStruct((m, d), F32)],
        compiler_params=_params("arbitrary", "arbitrary"),
        name="in_odd",
    )(h, w_in, w_in, w_in)


CONV_HALO = 32


def _conv_kernel(cur_ref, prev_ref, ctx_ref, zg_ref, w_ref, b_ref, lg_ref, lb_ref, o_ref, xp_s, y_s,
                 *, row_chunk):
    t = pl.program_id(1)
    tm, c = cur_ref.shape

    @pl.when(t == 0)
    def _():
        xp_s[0:CONV_HALO, :] = ctx_ref[0]

    @pl.when(t > 0)
    def _():
        xp_s[0:CONV_HALO, :] = prev_ref[...]

    xp_s[CONV_HALO:CONV_HALO + tm, :] = cur_ref[...]
    xp_s[CONV_HALO + tm:, :] = jnp.zeros((SUBLANES, c), F32)

    first = CONV_HALO - (CONV_W - 1)
    span = row_chunk + CONV_HALO

    def lane_body(lc, carry):
        c0 = pl.multiple_of(lc * LANES, LANES)

        def row_body(rc, carry2):
            r0 = pl.multiple_of(rc * row_chunk, row_chunk)
            blk = xp_s[pl.ds(r0, span + SUBLANES), pl.ds(c0, LANES)]
            acc = jnp.zeros((row_chunk, LANES), F32)
            for phase in range(SUBLANES):
                sh = blk[phase:phase + span]
                for a in range(span // SUBLANES):
                    tap = SUBLANES * a + phase - first
                    if 0 <= tap < CONV_W:
                        acc = acc + w_ref[pl.ds(tap, 1), pl.ds(c0, LANES)] * sh[SUBLANES * a:SUBLANES * a + row_chunk]
            y_s[pl.ds(r0, row_chunk), pl.ds(c0, LANES)] = acc
            return carry2

        return lax.fori_loop(0, tm // row_chunk, row_body, carry)

    lax.fori_loop(0, c // LANES, lane_body, 0)

    y = y_s[...] + b_ref[...]
    mu = jnp.mean(y, axis=-1, keepdims=True)
    yc = y - mu
    var = jnp.mean(yc * yc, axis=-1, keepdims=True)
    yn = yc * lax.rsqrt(var + EPS) * lg_ref[...] + lb_ref[...]
    o_ref[...] = (_silu(yn) * zg_ref[...]).astype(BF16)


def _conv_module(g, ctx, zg, conv_w, conv_b, ln_g, ln_b, nb, lb):
    c = g.shape[1]
    tm = min(256, lb)
    nt = lb // tm
    halo_blocks = tm // CONV_HALO
    w_pad = jnp.zeros((CONV_HALO, c), F32).at[:CONV_W].set(conv_w)
    cur = pl.BlockSpec((tm, c), lambda b, t: (b * nt + t, 0))
    prev = pl.BlockSpec((CONV_HALO, c), lambda b, t: (jnp.maximum((b * nt + t) * halo_blocks - 1, 0), 0))
    vec = pl.BlockSpec((1, c), lambda b, t: (0, 0))
    return pl.pallas_call(
        functools.partial(_conv_kernel, row_chunk=min(64, tm)),
        grid=(nb, nt),
        in_specs=[cur, prev, pl.BlockSpec((1, CONV_HALO, c), lambda b, t: (b, 0, 0)), cur,
                  pl.BlockSpec((CONV_HALO, c), lambda b, t: (0, 0)), vec, vec, vec],
        out_specs=cur,
        out_shape=jax.ShapeDtypeStruct((nb * lb, c), BF16),
        scratch_shapes=[pltpu.VMEM((CONV_HALO + tm + SUBLANES, c), F32), pltpu.VMEM((tm, c), F32)],
        compiler_params=_params("arbitrary", "arbitrary"),
        name="conv_module",
    )(g, g, ctx, zg, w_pad, conv_b.reshape(1, c), ln_g.reshape(1, c), ln_b.reshape(1, c))


def kernel(x_prompt, x_sample, c_prompt, c_sample, cache_k, cache_v, state_s5_re, state_s5_im, state_conv, norm_g, w_ada, b_ada, w_in_even, w_out_even, s5_lam_re, s5_lam_im, s5_log_dt, s5_b_re, s5_b_im, s5_c_re, s5_c_im, s5_d, s5_w_glu, q_norm_g, k_norm_g, lam_q1, lam_k1, lam_q2, lam_k2, attn_out_g, w_in_odd, conv_w, conv_b, conv_ln_g, conv_ln_b, w_out_odd):
    bp, lp, d = x_prompt.shape
    bs, ls, _ = x_sample.shape
    assert bp == 1 and w_ada.shape[0] == 2, "one prompt sequence, one even and one odd layer"
    wdt = d // 2
    heads = wdt // HEAD_DIM
    ns = (wdt // S5_GROUP) * S5_STATE
    ms = bs * ls

    xp = x_prompt.reshape(lp, d)
    xs = x_sample.reshape(ms, d)

    rows = bp + bs
    rows_pad = -(-rows // 16) * 16
    c_all = jnp.concatenate([c_prompt, c_sample, jnp.zeros((rows_pad - rows, d), F32)], axis=0)
    mod = _ada_mod(c_all, w_ada, b_ada)
    mod_p = [mod[l, :bp] for l in range(2)]
    mod_s = [jnp.repeat(mod[l, bp:rows], ls, axis=0) for l in range(2)]

    w_in_e = w_in_even[0].astype(BF16)
    w_out_e = w_out_even[0].astype(BF16)
    w_glu = s5_w_glu[0].astype(BF16)
    w_in_o = w_in_odd[0].astype(BF16)
    w_out_o = w_out_odd[0].astype(BF16)

    s5p = _s5_params(s5_lam_re[0], s5_lam_im[0], s5_log_dt[0], s5_b_re[0], s5_b_im[0],
                     s5_c_re[0], s5_c_im[0], s5_d[0])
    lam_init = 0.8 - 0.6 * math.exp(-0.3 * 0)
    lam = (jnp.exp(jnp.sum(lam_q1[0] * lam_k1[0])) - jnp.exp(jnp.sum(lam_q2[0] * lam_k2[0]))
           + lam_init).reshape(1).astype(F32)

    def even_layer(x, modl, nb, lb, h0r, h0i, cache):
        h = _modnorm(x, modl, norm_g[0])
        u, zs, q, k, kb, v, vb, za = _even_in_proj(h, w_in_e, q_norm_g[0], k_norm_g[0])
        gy, hr, hi = _s5(u, h0r, h0i, s5p, nb, lb)
        s5_out = _glu_proj(gy, w_glu, zs)
        if cache is None:
            att = _attention_prompt(q, kb, vb, za, attn_out_g[0], lam, lam_init)
        else:
            att = _attention_cached(q, k, v, cache[0], cache[1], za, attn_out_g[0], lam, lam_init, nb, lb)
        y = _out_proj("out_even", [s5_out, att], w_out_e, x, modl)
        return y, k, v, hr, hi

    def odd_layer(x, modl, nb, lb, ctx):
        h = _modnorm(x, modl, norm_g[1])
        g, zg = _odd_in_proj(h, w_in_o)
        yc = _conv_module(g, ctx, zg, conv_w[0], conv_b[0], conv_ln_g[0], conv_ln_b[0], nb, lb)
        y = _out_proj("out_odd", [yc], w_out_o, x, modl)
        return y, g

    zero_state = jnp.zeros((bp, 1, ns), F32)
    yp, kp, vp, hrp, hip = even_layer(xp, mod_p[0], bp, lp, zero_state, zero_state, None)
    past = cache_k.shape[2]
    ys, ksn, vsn, hrs, his = even_layer(
        xs, mod_s[0], bs, ls, state_s5_re[0].reshape(bs, 1, ns), state_s5_im[0].reshape(bs, 1, ns),
        (cache_k[0].reshape(bs, past, wdt), cache_v[0].reshape(bs, past, wdt)))

    pad = CONV_HALO - (CONV_W - 1)
    ctx_p = jnp.zeros((bp, CONV_HALO, d), F32)
    ctx_s = jnp.pad(state_conv[0], ((0, 0), (pad, 0), (0, 0)))
    yp, gp = odd_layer(yp, mod_p[1], bp, lp, ctx_p)
    ys, gs = odd_layer(ys, mod_s[1], bs, ls, ctx_s)

    groups = wdt // S5_GROUP
    tail = CONV_W - 1
    conv_p = gp.reshape(bp, lp, d)[:, lp - tail:]
    conv_s = jnp.concatenate([state_conv[0], gs.reshape(bs, ls, d)], axis=1)[:, -tail:]
    return (yp.reshape(bp, lp, d), ys.reshape(bs, ls, d),
            kp.reshape(1, bp, lp, heads, HEAD_DIM), vp.reshape(1, bp, lp, heads, HEAD_DIM),
            hrp.reshape(1, bp, groups, S5_STATE), hip.reshape(1, bp, groups, S5_STATE), conv_p[None],
            ksn.reshape(1, bs, ls, heads, HEAD_DIM), vsn.reshape(1, bs, ls, heads, HEAD_DIM),
            hrs.reshape(1, bs, groups, S5_STATE), his.reshape(1, bs, groups, S5_STATE), conv_s[None])
```

```python
import functools
import math

import jax
import jax.numpy as jnp
from jax import lax
from jax.experimental import pallas as pl
from jax.experimental.pallas import tpu as pltpu

F32 = jnp.float32
BF16 = jnp.bfloat16

CHUNK = 64
HEAD_DIM = 128
QK_DIM = 64
S5_GROUP = 16
S5_STATE = 64
CONV_W = 31
EPS = 1e-6
NEG_INF = -1e30
LOG2E = 1.4426950408889634

LANES = 128
SUBLANES = 8
ATT_BLOCK = 512
S5_LANE_GROUPS = LANES // S5_GROUP
S5_TILE_STATES = S5_LANE_GROUPS * S5_STATE
VMEM_LIMIT = 48 * 1024 * 1024


def _params(*sem):
    return pltpu.CompilerParams(dimension_semantics=sem, vmem_limit_bytes=VMEM_LIMIT)


def _silu(x):
    return x * jax.nn.sigmoid(x)


def _ada_kernel(c_ref, w_ref, b_ref, o_ref):
    a = _silu(c_ref[...]).astype(BF16)
    w = w_ref[0].astype(BF16)
    o_ref[0] = jnp.dot(a, w, preferred_element_type=F32) + b_ref[0]


def _ada_mod(c_all, w_ada, b_ada):
    depth, d, n = w_ada.shape
    rows = c_all.shape[0]
    tn = 512
    return pl.pallas_call(
        _ada_kernel,
        grid=(depth, n // tn),
        in_specs=[pl.BlockSpec((rows, d), lambda l, j: (0, 0)),
                  pl.BlockSpec((1, d, tn), lambda l, j: (l, 0, j)),
                  pl.BlockSpec((1, 1, tn), lambda l, j: (l, 0, j))],
        out_specs=pl.BlockSpec((1, rows, tn), lambda l, j: (l, 0, j)),
        out_shape=jax.ShapeDtypeStruct((depth, rows, n), F32),
        compiler_params=_params("arbitrary", "arbitrary"),
        name="ada_mod",
    )(c_all, w_ada, b_ada.reshape(depth, 1, n))


def _modnorm_kernel(x_ref, sh_ref, sc_ref, g_ref, o_ref):
    x = x_ref[...]
    ms = jnp.mean(x * x, axis=-1, keepdims=True)
    y = x * lax.rsqrt(ms + EPS) * g_ref[...]
    o_ref[...] = (y * (1.0 + sc_ref[...]) + sh_ref[...]).astype(BF16)


def _row_spec(arr, tm, width, col_block):
    if arr.shape[0] == 1:
        return pl.BlockSpec((1, width), lambda i, c=col_block: (0, c))
    return pl.BlockSpec((tm, width), lambda i, c=col_block: (i, c))


def _modnorm(x, mod, g):
    m, d = x.shape
    tm = min(512, m)
    return pl.pallas_call(
        _modnorm_kernel,
        grid=(m // tm,),
        in_specs=[pl.BlockSpec((tm, d), lambda i: (i, 0)),
                  _row_spec(mod, tm, d, 0),
                  _row_spec(mod, tm, d, 1),
                  pl.BlockSpec((1, d), lambda i: (0, 0))],
        out_specs=pl.BlockSpec((tm, d), lambda i: (i, 0)),
        out_shape=jax.ShapeDtypeStruct((m, d), BF16),
        compiler_params=_params("arbitrary"),
        name="modnorm",
    )(x, mod, mod, g.reshape(1, d))


def _linear(name, xs, w, col0, ncols, ncb, epilogue, extras, outs, tm=512):
    m = xs[0].shape[0]
    tm = min(tm, m)
    k_total = w.shape[0]
    nx, ne = len(xs), len(extras)

    def kernel(*refs):
        x_refs, w_ref = refs[:nx], refs[nx]
        e_refs, o_refs = refs[nx + 1:nx + 1 + ne], refs[nx + 1 + ne:]
        acc, k0 = None, 0
        for xr in x_refs:
            kp = xr.shape[1]
            part = jnp.dot(xr[...], w_ref[k0:k0 + kp, :], preferred_element_type=F32)
            acc = part if acc is None else acc + part
            k0 += kp
        epilogue(acc, e_refs, o_refs)

    in_specs = [pl.BlockSpec((tm, x.shape[1]), lambda n, i: (i, 0)) for x in xs]
    in_specs.append(pl.BlockSpec((k_total, ncols), lambda n, i: (0, col0 + n)))
    in_specs += [pl.BlockSpec(bs, im) for _, bs, im in extras]
    res = pl.pallas_call(
        kernel,
        grid=(ncb, m // tm),
        in_specs=in_specs,
        out_specs=[pl.BlockSpec(bs, im) for _, _, bs, im in outs],
        out_shape=[jax.ShapeDtypeStruct(s, dt) for s, dt, _, _ in outs],
        compiler_params=_params("arbitrary", "arbitrary"),
        name=name,
    )(*xs, w, *[a for a, _, _ in extras])
    return res


def _half_mean_matrix():
    r = lax.broadcasted_iota(jnp.int32, (HEAD_DIM, HEAD_DIM), 0) // QK_DIM
    c = lax.broadcasted_iota(jnp.int32, (HEAD_DIM, HEAD_DIM), 1) // QK_DIM
    return jnp.where(r == c, 1.0 / QK_DIM, 0.0).astype(BF16)


def _qk_norm(acc, g_row, s_mat):
    heads = acc.shape[1] // HEAD_DIM
    outs = []
    for h in range(heads):
        xh = acc[:, h * HEAD_DIM:(h + 1) * HEAD_DIM]
        t = xh * xh
        hi = t.astype(BF16)
        lo = (t - hi.astype(F32)).astype(BF16)
        ms = (jnp.dot(hi, s_mat, preferred_element_type=F32)
              + jnp.dot(lo, s_mat, preferred_element_type=F32))
        outs.append(xh * lax.rsqrt(ms + EPS))
    return jnp.concatenate(outs, axis=1) * g_row


ALIBI_LANES = 3


def _split_maps(y, aug_fn):
    tm = y.shape[0]
    lane = lax.broadcasted_iota(jnp.int32, (tm, HEAD_DIM), 1)
    first, second = [], []
    for h in range(y.shape[1] // HEAD_DIM):
        yh = y[:, h * HEAD_DIM:(h + 1) * HEAD_DIM]
        aug = aug_fn(h, lane)
        first.append(jnp.where(lane < QK_DIM, yh, aug))
        second.append(jnp.where(lane < QK_DIM, pltpu.roll(yh, QK_DIM, 1), aug))
    return jnp.concatenate(first, axis=1), jnp.concatenate(second, axis=1)


def _q_aug(h, lane):
    return jnp.where(lane < QK_DIM + ALIBI_LANES, 1.0, 0.0)


def _k_aug(h, lane, heads):
    slope = LOG2E * 2.0 ** (-8.0 * (h + 1) / heads)
    b = slope * lax.broadcasted_iota(jnp.int32, lane.shape, 0).astype(F32)
    hi = b.astype(BF16).astype(F32)
    mid = (b - hi).astype(BF16).astype(F32)
    lo = b - hi - mid
    return jnp.where(lane == QK_DIM, hi, jnp.where(lane == QK_DIM + 1, mid,
                                                   jnp.where(lane == QK_DIM + 2, lo, 0.0)))


def _even_in_proj(h, w_in, q_g, k_g, split_maps):
    m = h.shape[0]
    wdt = w_in.shape[1] // 6
    heads = wdt // HEAD_DIM
    tm = min(ATT_BLOCK, m)
    s_mat = _half_mean_matrix()
    row = lambda n, i: (i, 0)
    fix = lambda n, i: (0, 0)
    f32_out = ((m, wdt), F32, (tm, wdt), row)
    bf_out = ((m, wdt), BF16, (tm, wdt), row)
    qg = jnp.tile(q_g, heads).reshape(1, wdt)
    kg = jnp.tile(k_g, heads).reshape(1, wdt)
    norm_extras = lambda g: [(g, (1, wdt), fix), (s_mat, (HEAD_DIM, HEAD_DIM), fix)]

    def ep_plain(acc, e, o):
        o[0][...] = acc

    def ep_silu(acc, e, o):
        o[0][...] = _silu(acc)

    def ep_q(acc, e, o):
        y = _qk_norm(acc, e[0][...], e[1][...]) * (LOG2E * QK_DIM ** -0.5)
        if split_maps:
            y1, y2 = _split_maps(y, _q_aug)
            o[0][...] = y1.astype(BF16)
            o[1][...] = y2.astype(BF16)
        else:
            o[0][...] = y.astype(BF16)

    def ep_k(acc, e, o):
        y = _qk_norm(acc, e[0][...], e[1][...])
        o[0][...] = y
        if split_maps:
            y1, y2 = _split_maps(y, functools.partial(_k_aug, heads=heads))
            o[1][...] = y1.astype(BF16)
            o[2][...] = y2.astype(BF16)

    def ep_v(acc, e, o):
        o[0][...] = acc
        if split_maps:
            o[1][...] = acc.astype(BF16)

    maps = [bf_out, bf_out] if split_maps else []
    (u,) = _linear("in_even_u", [h], w_in, 0, wdt, 1, ep_plain, [], [f32_out], tm)
    (zs,) = _linear("in_even_zs", [h], w_in, 1, wdt, 1, ep_silu, [], [f32_out], tm)
    q = _linear("in_even_q", [h], w_in, 2, wdt, 1, ep_q, norm_extras(qg), maps or [bf_out], tm)
    k = _linear("in_even_k", [h], w_in, 3, wdt, 1, ep_k, norm_extras(kg), [f32_out] + maps, tm)
    v = _linear("in_even_v", [h], w_in, 4, wdt, 1, ep_v, [], [f32_out] + maps[:1], tm)
    (za,) = _linear("in_even_za", [h], w_in, 5, wdt, 1, ep_silu, [], [f32_out], tm)
    return u, zs, q, k, v, za


def _glu_proj(gy, w_glu, zs_gate):
    m, wdt = gy.shape
    tm = min(512, m)

    def ep(acc, e, o):
        o[0][...] = (acc[:, :wdt] * jax.nn.sigmoid(acc[:, wdt:]) * e[0][...]).astype(BF16)

    (out,) = _linear("s5_glu", [gy], w_glu, 0, 2 * wdt, 1, ep,
                     [(zs_gate, (tm, wdt), lambda n, i: (i, 0))],
                     [((m, wdt), BF16, (tm, wdt), lambda n, i: (i, 0))])
    return out


def _out_proj(name, xs, w, x_res, mod):
    m, d = x_res.shape
    tm = min(512, m)
    tn = min(1024, d)
    ncb = d // tn
    gate_col = 2 * ncb
    if mod.shape[0] == 1:
        gate = (mod, (1, tn), lambda n, i: (0, gate_col + n))
    else:
        gate = (mod, (tm, tn), lambda n, i: (i, gate_col + n))

    def ep(acc, e, o):
        o[0][...] = e[0][...] + e[1][...] * acc

    (y,) = _linear(name, xs, w, 0, tn, ncb, ep,
                   [(x_res, (tm, tn), lambda n, i: (i, n)), gate],
                   [((m, d), F32, (tm, tn), lambda n, i: (i, n))])
    return y


def _s5_kernel(u_ref, h0r_ref, h0i_ref, wbr_ref, wbi_ref, wcr_ref, wci_ref, d_ref, ar_ref, ai_ref,
               gy_ref, hr_ref, hi_ref, xr_s, xi_s, sr_s, si_s, coef_s, *, lane_chunk):
    b, t = pl.program_id(0), pl.program_id(1)
    rows, ns = xr_s.shape
    nc = wbr_ref.shape[0]

    @pl.when((b == 0) & (t == 0))
    def _():
        row = lax.broadcasted_iota(jnp.int32, (SUBLANES, ns), 0)
        a1 = (jnp.broadcast_to(ar_ref[...], (SUBLANES, ns)), jnp.broadcast_to(ai_ref[...], (SUBLANES, ns)))
        cmul = lambda x, y: (x[0] * y[0] - x[1] * y[1], x[0] * y[1] + x[1] * y[0])
        a2 = cmul(a1, a1)
        a4 = cmul(a2, a2)
        pc = a1
        for k in range(1, SUBLANES):
            nxt = cmul(pc, a1)
            pc = (jnp.where(row >= k, nxt[0], pc[0]), jnp.where(row >= k, nxt[1], pc[1]))
        for idx, (p, dist) in enumerate(((a1, 1), (a2, 2), (a4, 4))):
            coef_s[2 * idx] = jnp.where(row >= dist, p[0], 0.0)
            coef_s[2 * idx + 1] = jnp.where(row >= dist, p[1], 0.0)
        coef_s[6] = pc[0]
        coef_s[7] = pc[1]

    @pl.when(t == 0)
    def _():
        sr_s[...] = h0r_ref[0]
        si_s[...] = h0i_ref[0]

    ub = u_ref[...].astype(BF16)
    for c in range(nc):
        uc = ub[:, c * LANES:(c + 1) * LANES]
        cs = slice(c * S5_TILE_STATES, (c + 1) * S5_TILE_STATES)
        xr_s[:, cs] = jnp.dot(uc, wbr_ref[c], preferred_element_type=F32)
        xi_s[:, cs] = jnp.dot(uc, wbi_ref[c], preferred_element_type=F32)

    for lc in range(ns // lane_chunk):
        ls = slice(lc * lane_chunk, (lc + 1) * lane_chunk)
        m1r, m1i, m2r, m2i, m4r, m4i, pcr, pci = [coef_s[k, :, ls] for k in range(8)]

        def tile(ti, carry, ls=ls, m1r=m1r, m1i=m1i, m2r=m2r, m2i=m2i, m4r=m4r, m4i=m4i,
                 pcr=pcr, pci=pci):
            cr, ci = carry
            r0 = pl.multiple_of(ti * SUBLANES, SUBLANES)
            a = xr_s[pl.ds(r0, SUBLANES), ls]
            bb = xi_s[pl.ds(r0, SUBLANES), ls]
            for dist, mr, mi in ((1, m1r, m1i), (2, m2r, m2i), (4, m4r, m4i)):
                sa = pltpu.roll(a, dist, 0)
                sb = pltpu.roll(bb, dist, 0)
                a, bb = a + (mr * sa - mi * sb), bb + (mr * sb + mi * sa)
            a, bb = a + (pcr * cr - pci * ci), bb + (pcr * ci + pci * cr)
            xr_s[pl.ds(r0, SUBLANES), ls] = a
            xi_s[pl.ds(r0, SUBLANES), ls] = bb
            return a[SUBLANES - 1:SUBLANES, :], bb[SUBLANES - 1:SUBLANES, :]

        cr, ci = lax.fori_loop(0, rows // SUBLANES, tile, (sr_s[:, ls], si_s[:, ls]))
        sr_s[:, ls] = cr
        si_s[:, ls] = ci

    for c in range(nc):
        cs = slice(c * S5_TILE_STATES, (c + 1) * S5_TILE_STATES)
        os_ = slice(c * LANES, (c + 1) * LANES)
        y = (jnp.dot(xr_s[:, cs].astype(BF16), wcr_ref[c], preferred_element_type=F32)
             - jnp.dot(xi_s[:, cs].astype(BF16), wci_ref[c], preferred_element_type=F32)
             + d_ref[:, os_] * u_ref[:, os_])
        gy_ref[:, os_] = jax.nn.gelu(y).astype(BF16)

    @pl.when(t == pl.num_programs(1) - 1)
    def _():
        hr_ref[0] = sr_s[...]
        hi_ref[0] = si_s[...]


def _s5(u, h0r, h0i, prm, nb, lb):
    wdt = u.shape[1]
    ns = h0r.shape[-1]
    nc = wdt // LANES
    rows = min(256, lb)
    nt = lb // rows
    full = lambda a: pl.BlockSpec(a.shape, lambda b, t, nd=a.ndim: (0,) * nd)
    st = pl.BlockSpec((1, 1, ns), lambda b, t: (b, 0, 0))
    tok = pl.BlockSpec((rows, wdt), lambda b, t: (b * nt + t, 0))
    return pl.pallas_call(
        functools.partial(_s5_kernel, lane_chunk=min(512, ns)),
        grid=(nb, nt),
        in_specs=[tok, st, st] + [full(prm[k]) for k in ("wbr", "wbi", "wcr", "wci", "d", "ar", "ai")],
        out_specs=[tok, st, st],
        out_shape=[jax.ShapeDtypeStruct((nb * lb, wdt), BF16),
                   jax.ShapeDtypeStruct((nb, 1, ns), F32),
                   jax.ShapeDtypeStruct((nb, 1, ns), F32)],
        scratch_shapes=[pltpu.VMEM((rows, ns), F32), pltpu.VMEM((rows, ns), F32),
                        pltpu.VMEM((1, ns), F32), pltpu.VMEM((1, ns), F32),
                        pltpu.VMEM((8, SUBLANES, ns), F32)],
        compiler_params=_params("arbitrary", "arbitrary"),
        name="s5_scan",
    )(u, h0r, h0i, prm["wbr"], prm["wbi"], prm["wcr"], prm["wci"], prm["d"], prm["ar"], prm["ai"])


def _s5_params(lam_re, lam_im, log_dt, b_re, b_im, c_re, c_im, d_skip):
    g, p = lam_re.shape
    nc = g // S5_LANE_GROUPS
    dt = jnp.exp(log_dt)[:, None]
    mag = jnp.exp(lam_re * dt)
    ar = mag * jnp.cos(lam_im * dt)
    ai = mag * jnp.sin(lam_im * dt)
    den = lam_re * lam_re + lam_im * lam_im
    zr = ((ar - 1.0) * lam_re + ai * lam_im) / den
    zi = (ai * lam_re - (ar - 1.0) * lam_im) / den
    bbr = zr[..., None] * b_re - zi[..., None] * b_im
    bbi = zr[..., None] * b_im + zi[..., None] * b_re
    eye = jnp.eye(S5_LANE_GROUPS, dtype=F32)

    def bdiag_in(x):
        t = x.reshape(nc, S5_LANE_GROUPS, p, S5_GROUP).transpose(0, 1, 3, 2)
        return (t[:, :, :, None, :] * eye[None, :, None, :, None]).reshape(nc, LANES, S5_TILE_STATES).astype(BF16)

    def bdiag_out(x):
        t = x.reshape(nc, S5_LANE_GROUPS, S5_GROUP, p).transpose(0, 1, 3, 2)
        return (t[:, :, :, None, :] * eye[None, :, None, :, None]).reshape(nc, S5_TILE_STATES, LANES).astype(BF16)

    return dict(wbr=bdiag_in(bbr), wbi=bdiag_in(bbi), wcr=bdiag_out(c_re), wci=bdiag_out(c_im),
                d=d_skip.reshape(1, g * S5_GROUP), ar=ar.reshape(1, g * p), ai=ai.reshape(1, g * p))


def _stack_maps(q):
    lane = lax.broadcasted_iota(jnp.int32, q.shape, 1)
    zero = jnp.zeros_like(q)
    return jnp.concatenate([jnp.where(lane < QK_DIM, q, zero), jnp.where(lane >= QK_DIM, q, zero)], axis=0)


def _qk(qq, kb):
    return lax.dot_general(qq, kb, (((1,), (1,)), ((), ())), preferred_element_type=F32)


def _attn_finish(o, g_row, gate, lam_init):
    ms = jnp.mean(o * o, axis=-1, keepdims=True)
    return (o * lax.rsqrt(ms + EPS) * g_row * (1.0 - lam_init) * gate).astype(BF16)


def _attn_kernel(slopes_ref, lam_ref, q1_ref, q2_ref, k1_ref, k2_ref, vt_ref, za_ref, og_ref, o_ref,
                 m_s, l_s, acc_s, sa_s, sb_s, ma_s, mb_s, dbias_s, *, lam_init):
    h, i = pl.program_id(0), pl.program_id(1)
    t = q1_ref.shape[0]
    nq = 2 * t
    slope = slopes_ref[h]

    @pl.when(i == 0)
    def _():
        k_local = lax.broadcasted_iota(jnp.int32, (t, nq), 0)
        q_col = lax.broadcasted_iota(jnp.int32, (t, nq), 1)
        q_local = jnp.where(q_col >= t, q_col - t, q_col)
        vis = (k_local // CHUNK) <= (q_local // CHUNK)
        dist = jnp.abs(q_local - k_local).astype(F32)
        dbias_s[...] = (jnp.where(vis, -slope * dist, NEG_INF)
                        + slope * (q_local - k_local).astype(F32))

    q1 = q1_ref[...]
    q2 = q2_ref[...]
    m_s[...] = jnp.full(m_s.shape, -jnp.inf, F32)
    l_s[...] = jnp.zeros(l_s.shape, F32)
    acc_s[...] = jnp.zeros(acc_s.shape, F32)

    def scores(blk, diagonal, s_ref, mx_ref):
        r0 = pl.multiple_of(blk * t, t)
        for half, (k_ref, q) in enumerate(((k1_ref, q1), (k2_ref, q2))):
            cols = slice(half * t, (half + 1) * t)
            s_t = _qk(k_ref[pl.ds(r0, t), :], q)
            if diagonal:
                s_t = s_t + dbias_s[:, cols]
            s_ref[:, cols] = s_t
            mx_ref[:, cols] = jnp.max(s_t, axis=0, keepdims=True)

    def consume(blk, s_ref, mx_ref):
        r0 = pl.multiple_of(blk * t, t)
        shift = slope * ((blk - i) * t).astype(F32)
        m_old = m_s[...]
        m_new = jnp.maximum(m_old, mx_ref[...] + shift)
        alpha = jnp.exp2(m_old - m_new)
        p = jnp.exp2(s_ref[...] - (m_new - shift))
        l_s[...] = alpha * l_s[...] + jnp.sum(p, axis=0, keepdims=True)
        acc_s[...] = alpha * acc_s[...] + jnp.dot(vt_ref[:, pl.ds(r0, t)], p.astype(BF16),
                                                  preferred_element_type=F32)
        m_s[...] = m_new

    scores(i, True, sa_s, ma_s)
    pairs = i // 2

    def pair(jj, carry):
        j0 = 2 * jj
        scores(j0, False, sb_s, mb_s)
        consume(jnp.where(jj == 0, i, j0 - 1), sa_s, ma_s)
        scores(j0 + 1, False, sa_s, ma_s)
        consume(j0, sb_s, mb_s)
        return carry

    lax.fori_loop(0, pairs, pair, 0)
    in_a = jnp.where(pairs == 0, i, 2 * pairs - 1)

    @pl.when(i % 2 == 1)
    def _():
        scores(i - 1, False, sb_s, mb_s)
        consume(in_a, sa_s, ma_s)
        consume(i - 1, sb_s, mb_s)

    @pl.when(i % 2 == 0)
    def _():
        consume(in_a, sa_s, ma_s)

    acc = acc_s[...]
    l = l_s[...]
    o_t = acc[:, :t] / l[:, :t] - lam_ref[0] * (acc[:, t:] / l[:, t:])
    ms = jnp.mean(o_t * o_t, axis=0, keepdims=True)
    y = (o_t * lax.rsqrt(ms + EPS)).T
    o_ref[...] = (y * og_ref[...] * (1.0 - lam_init) * za_ref[...]).astype(BF16)


def _attention_prompt(q1, q2, k1, k2, vb, za_gate, out_g, lam, lam_init):
    m, wdt = q1.shape
    heads = wdt // HEAD_DIM
    t = min(ATT_BLOCK, m)
    slopes = LOG2E * 2.0 ** (-8.0 * jnp.arange(1, heads + 1, dtype=F32) / heads)
    smem = pl.BlockSpec(memory_space=pltpu.SMEM)
    blk = pl.BlockSpec((t, HEAD_DIM), lambda h, i: (i, h))
    keys = pl.BlockSpec((m, HEAD_DIM), lambda h, i: (0, h))
    return pl.pallas_call(
        functools.partial(_attn_kernel, lam_init=lam_init),
        grid=(heads, m // t),
        in_specs=[smem, smem, blk, blk, keys, keys,
                  pl.BlockSpec((HEAD_DIM, m), lambda h, i: (h, 0)),
                  blk, pl.BlockSpec((1, HEAD_DIM), lambda h, i: (0, 0))],
        out_specs=blk,
        out_shape=jax.ShapeDtypeStruct((m, wdt), BF16),
        scratch_shapes=[pltpu.VMEM((1, 2 * t), F32), pltpu.VMEM((1, 2 * t), F32),
                        pltpu.VMEM((HEAD_DIM, 2 * t), F32),
                        pltpu.VMEM((t, 2 * t), F32), pltpu.VMEM((t, 2 * t), F32),
                        pltpu.VMEM((1, 2 * t), F32), pltpu.VMEM((1, 2 * t), F32),
                        pltpu.VMEM((t, 2 * t), F32)],
        compiler_params=_params("arbitrary", "arbitrary"),
        name="attn_prompt",
    )(slopes, lam, q1, q2, k1, k2, vb.T, za_gate, out_g.reshape(1, HEAD_DIM))


def _attn_cached_kernel(lam_ref, q_ref, kn_ref, vn_ref, kc_ref, vc_ref, za_ref, og_ref, o_ref,
                        *, lam_init, heads):
    ls = q_ref.shape[0]
    past = kc_ref.shape[1]
    lam = lam_ref[0]
    ql = lax.broadcasted_iota(jnp.int32, (2 * ls, 1), 0)
    q_pos = past + jnp.where(ql >= ls, ql - ls, ql)
    kc_pos = lax.broadcasted_iota(jnp.int32, (1, past), 1)
    kn_pos = past + lax.broadcasted_iota(jnp.int32, (1, ls), 1)

    def bias(k_pos, slope):
        vis = (k_pos // CHUNK) <= (q_pos // CHUNK)
        return jnp.where(vis, -slope * jnp.abs(q_pos - k_pos).astype(F32), NEG_INF)

    for h in range(heads):
        hs = slice(h * HEAD_DIM, (h + 1) * HEAD_DIM)
        slope = LOG2E * 2.0 ** (-8.0 * (h + 1) / heads)
        qq = _stack_maps(q_ref[:, hs])
        sc = _qk(qq, kc_ref[0, :, hs]) + bias(kc_pos, slope)
        sn = _qk(qq, kn_ref[:, hs].astype(BF16)) + bias(kn_pos, slope)
        mx = jnp.maximum(jnp.max(sc, axis=1, keepdims=True), jnp.max(sn, axis=1, keepdims=True))
        pc = jnp.exp2(sc - mx)
        pn = jnp.exp2(sn - mx)
        inv = 1.0 / (jnp.sum(pc, axis=1, keepdims=True) + jnp.sum(pn, axis=1, keepdims=True))
        pc = pc * inv
        pn = pn * inv
        wc = (pc[:ls] - lam * pc[ls:]).astype(BF16)
        wn = (pn[:ls] - lam * pn[ls:]).astype(BF16)
        o = (jnp.dot(wc, vc_ref[0, :, hs], preferred_element_type=F32)
             + jnp.dot(wn, vn_ref[:, hs].astype(BF16), preferred_element_type=F32))
        o_ref[:, hs] = _attn_finish(o, og_ref[...], za_ref[:, hs], lam_init)


def _attention_cached(q, k_new, v_new, cache_k, cache_v, za_gate, out_g, lam, lam_init, nb, ls):
    wdt = q.shape[1]
    heads = wdt // HEAD_DIM
    past = cache_k.shape[1]
    tok = pl.BlockSpec((ls, wdt), lambda b: (b, 0))
    cache = pl.BlockSpec((1, past, wdt), lambda b: (b, 0, 0))
    return pl.pallas_call(
        functools.partial(_attn_cached_kernel, lam_init=lam_init, heads=heads),
        grid=(nb,),
        in_specs=[pl.BlockSpec(memory_space=pltpu.SMEM), tok, tok, tok, cache, cache, tok,
                  pl.BlockSpec((1, HEAD_DIM), lambda b: (0, 0))],
        out_specs=tok,
        out_shape=jax.ShapeDtypeStruct((nb * ls, wdt), BF16),
        compiler_params=_params("arbitrary"),
        name="attn_cached",
    )(lam, q, k_new, v_new, cache_k, cache_v, za_gate, out_g.reshape(1, HEAD_DIM))


def _odd_in_proj(h, w_in):
    m, d = h.shape
    tm = min(512, m)
    tn = min(512, d)
    ncb = d // tn

    def kernel(h_ref, wa_ref, wb_ref, wz_ref, g_ref, z_ref):
        x = h_ref[...]
        a = jnp.dot(x, wa_ref[...], preferred_element_type=F32)
        b = jnp.dot(x, wb_ref[...], preferred_element_type=F32)
        z = jnp.dot(x, wz_ref[...], preferred_element_type=F32)
        g_ref[...] = a * jax.nn.sigmoid(b)
        z_ref[...] = _silu(z)

    wspec = lambda off: pl.BlockSpec((d, tn), lambda n, i, off=off: (0, n + off))
    out = pl.BlockSpec((tm, tn), lambda n, i: (i, n))
    return pl.pallas_call(
        kernel,
        grid=(ncb, m // tm),
        in_specs=[pl.BlockSpec((tm, d), lambda n, i: (i, 0)), wspec(0), wspec(ncb), wspec(2 * ncb)],
        out_specs=[out, out],
        out_shape=[jax.ShapeDtypeStruct((m, d), F32), jax.ShapeDtypeStruct((m, d), F32)],
        compiler_params=_params("arbitrary", "arbitrary"),
        name="in_odd",
    )(h, w_in, w_in, w_in)


CONV_HALO = 32


LN_ROWS = 64


def _conv_kernel(cur_ref, prev_ref, ctx_ref, zg_ref, w_ref, b_ref, lg_ref, lb_ref, o_ref, xp_s, y_s,
                 sh_s, *, row_chunk):
    t = pl.program_id(1)
    tm, c = cur_ref.shape

    @pl.when(t == 0)
    def _():
        xp_s[0:CONV_HALO, :] = ctx_ref[0]

    @pl.when(t > 0)
    def _():
        xp_s[0:CONV_HALO, :] = prev_ref[...]

    xp_s[CONV_HALO:CONV_HALO + tm, :] = cur_ref[...]
    xp_s[CONV_HALO + tm:, :] = jnp.zeros((SUBLANES, c), F32)

    first = CONV_HALO - (CONV_W - 1)
    span = row_chunk + CONV_HALO

    def lane_body(lc, carry):
        c0 = pl.multiple_of(lc * LANES, LANES)

        def row_body(rc, carry2):
            r0 = pl.multiple_of(rc * row_chunk, row_chunk)
            blk = xp_s[pl.ds(r0, span + SUBLANES), pl.ds(c0, LANES)]
            for phase in range(SUBLANES):
                sh_s[phase] = blk[phase:phase + span]
            acc = jnp.zeros((row_chunk, LANES), F32)
            for phase in range(SUBLANES):
                for a in range(span // SUBLANES):
                    tap = SUBLANES * a + phase - first
                    if 0 <= tap < CONV_W:
                        acc = acc + (w_ref[pl.ds(tap, 1), pl.ds(c0, LANES)]
                                     * sh_s[phase, SUBLANES * a:SUBLANES * a + row_chunk, :])
            y_s[pl.ds(r0, row_chunk), pl.ds(c0, LANES)] = acc
            return carry2

        return lax.fori_loop(0, tm // row_chunk, row_body, carry)

    lax.fori_loop(0, c // LANES, lane_body, 0)

    ln_rows = min(LN_ROWS, tm)

    def ln_body(r, carry):
        rows = pl.ds(pl.multiple_of(r * ln_rows, ln_rows), ln_rows)
        y = y_s[rows, :] + b_ref[...]
        mu = jnp.mean(y, axis=-1, keepdims=True)
        yc = y - mu
        var = jnp.mean(yc * yc, axis=-1, keepdims=True)
        yn = yc * lax.rsqrt(var + EPS) * lg_ref[...] + lb_ref[...]
        o_ref[rows, :] = (_silu(yn) * zg_ref[rows, :]).astype(BF16)
        return carry

    lax.fori_loop(0, tm // ln_rows, ln_body, 0)


def _conv_module(g, ctx, zg, conv_w, conv_b, ln_g, ln_b, nb, lb):
    c = g.shape[1]
    tm = min(256, lb)
    nt = lb // tm
    halo_blocks = tm // CONV_HALO
    w_pad = jnp.zeros((CONV_HALO, c), F32).at[:CONV_W].set(conv_w)
    cur = pl.BlockSpec((tm, c), lambda b, t: (b * nt + t, 0))
    prev = pl.BlockSpec((CONV_HALO, c), lambda b, t: (jnp.maximum((b * nt + t) * halo_blocks - 1, 0), 0))
    vec = pl.BlockSpec((1, c), lambda b, t: (0, 0))
    return pl.pallas_call(
        functools.partial(_conv_kernel, row_chunk=min(64, tm)),
        grid=(nb, nt),
        in_specs=[cur, prev, pl.BlockSpec((1, CONV_HALO, c), lambda b, t: (b, 0, 0)), cur,
                  pl.BlockSpec((CONV_HALO, c), lambda b, t: (0, 0)), vec, vec, vec],
        out_specs=cur,
        out_shape=jax.ShapeDtypeStruct((nb * lb, c), BF16),
        scratch_shapes=[pltpu.VMEM((CONV_HALO + tm + SUBLANES, c), F32), pltpu.VMEM((tm, c), F32),
                        pltpu.VMEM((SUBLANES, min(64, tm) + CONV_HALO, LANES), F32)],
        compiler_params=_params("arbitrary", "arbitrary"),
        name="conv_module",
    )(g, g, ctx, zg, w_pad, conv_b.reshape(1, c), ln_g.reshape(1, c), ln_b.reshape(1, c))


def kernel(x_prompt, x_sample, c_prompt, c_sample, cache_k, cache_v, state_s5_re, state_s5_im, state_conv, norm_g, w_ada, b_ada, w_in_even, w_out_even, s5_lam_re, s5_lam_im, s5_log_dt, s5_b_re, s5_b_im, s5_c_re, s5_c_im, s5_d, s5_w_glu, q_norm_g, k_norm_g, lam_q1, lam_k1, lam_q2, lam_k2, attn_out_g, w_in_odd, conv_w, conv_b, conv_ln_g, conv_ln_b, w_out_odd):
    bp, lp, d = x_prompt.shape
    bs, ls, _ = x_sample.shape
    assert bp == 1 and w_ada.shape[0] == 2, "one prompt sequence, one even and one odd layer"
    wdt = d // 2
    heads = wdt // HEAD_DIM
    ns = (wdt // S5_GROUP) * S5_STATE
    ms = bs * ls

    xp = x_prompt.reshape(lp, d)
    xs = x_sample.reshape(ms, d)

    rows = bp + bs
    rows_pad = -(-rows // 16) * 16
    c_all = jnp.concatenate([c_prompt, c_sample, jnp.zeros((rows_pad - rows, d), F32)], axis=0)
    mod = _ada_mod(c_all, w_ada, b_ada)
    mod_p = [mod[l, :bp] for l in range(2)]
    mod_s = [jnp.repeat(mod[l, bp:rows], ls, axis=0) for l in range(2)]

    w_in_e = w_in_even[0].astype(BF16)
    w_out_e = w_out_even[0].astype(BF16)
    w_glu = s5_w_glu[0].astype(BF16)
    w_in_o = w_in_odd[0].astype(BF16)
    w_out_o = w_out_odd[0].astype(BF16)

    s5p = _s5_params(s5_lam_re[0], s5_lam_im[0], s5_log_dt[0], s5_b_re[0], s5_b_im[0],
                     s5_c_re[0], s5_c_im[0], s5_d[0])
    lam_init = 0.8 - 0.6 * math.exp(-0.3 * 0)
    lam = (jnp.exp(jnp.sum(lam_q1[0] * lam_k1[0])) - jnp.exp(jnp.sum(lam_q2[0] * lam_k2[0]))
           + lam_init).reshape(1).astype(F32)

    def even_layer(x, modl, nb, lb, h0r, h0i, cache):
        h = _modnorm(x, modl, norm_g[0])
        u, zs, q, k, v, za = _even_in_proj(h, w_in_e, q_norm_g[0], k_norm_g[0], cache is None)
        gy, hr, hi = _s5(u, h0r, h0i, s5p, nb, lb)
        s5_out = _glu_proj(gy, w_glu, zs)
        if cache is None:
            att = _attention_prompt(q[0], q[1], k[1], k[2], v[1], za, attn_out_g[0], lam, lam_init)
        else:
            att = _attention_cached(q[0], k[0], v[0], cache[0], cache[1], za, attn_out_g[0], lam,
                                    lam_init, nb, lb)
        y = _out_proj("out_even", [s5_out, att], w_out_e, x, modl)
        return y, k[0], v[0], hr, hi

    def odd_layer(x, modl, nb, lb, ctx):
        h = _modnorm(x, modl, norm_g[1])
        g, zg = _odd_in_proj(h, w_in_o)
        yc = _conv_module(g, ctx, zg, conv_w[0], conv_b[0], conv_ln_g[0], conv_ln_b[0], nb, lb)
        y = _out_proj("out_odd", [yc], w_out_o, x, modl)
        return y, g

    zero_state = jnp.zeros((bp, 1, ns), F32)
    yp, kp, vp, hrp, hip = even_layer(xp, mod_p[0], bp, lp, zero_state, zero_state, None)
    past = cache_k.shape[2]
    ys, ksn, vsn, hrs, his = even_layer(
        xs, mod_s[0], bs, ls, state_s5_re[0].reshape(bs, 1, ns), state_s5_im[0].reshape(bs, 1, ns),
        (cache_k[0].astype(BF16).reshape(bs, past, wdt), cache_v[0].astype(BF16).reshape(bs, past, wdt)))

    pad = CONV_HALO - (CONV_W - 1)
    ctx_p = jnp.zeros((bp, CONV_HALO, d), F32)
    ctx_s = jnp.pad(state_conv[0], ((0, 0), (pad, 0), (0, 0)))
    yp, gp = odd_layer(yp, mod_p[1], bp, lp, ctx_p)
    ys, gs = odd_layer(ys, mod_s[1], bs, ls, ctx_s)

    groups = wdt // S5_GROUP
    tail = CONV_W - 1
    conv_p = gp.reshape(bp, lp, d)[:, lp - tail:]
    conv_s = jnp.concatenate([state_conv[0], gs.reshape(bs, ls, d)], axis=1)[:, -tail:]
    return (yp.reshape(bp, lp, d), ys.reshape(bs, ls, d),
            kp.reshape(1, bp, lp, heads, HEAD_DIM), vp.reshape(1, bp, lp, heads, HEAD_DIM),
            hrp.reshape(1, bp, groups, S5_STATE), hip.reshape(1, bp, groups, S5_STATE), conv_p[None],
            ksn.reshape(1, bs, ls, heads, HEAD_DIM), vsn.reshape(1, bs, ls, heads, HEAD_DIM),
            hrs.reshape(1, bs, groups, S5_STATE), his.reshape(1, bs, groups, S5_STATE), conv_s[None])
```

```python
import functools
import math

import jax
import jax.numpy as jnp
from jax import lax
from jax.experimental import pallas as pl
from jax.experimental.pallas import tpu as pltpu

F32 = jnp.float32
BF16 = jnp.bfloat16

CHUNK = 64
HEAD_DIM = 128
QK_DIM = 64
S5_GROUP = 16
S5_STATE = 64
CONV_W = 31
EPS = 1e-6
NEG_INF = -1e30
LOG2E = 1.4426950408889634

LANES = 128
SUBLANES = 8
ATT_BLOCK = 512
S5_LANE_GROUPS = LANES // S5_GROUP
S5_TILE_STATES = S5_LANE_GROUPS * S5_STATE
VMEM_LIMIT = 48 * 1024 * 1024


def _params(*sem):
    return pltpu.CompilerParams(dimension_semantics=sem, vmem_limit_bytes=VMEM_LIMIT)


def _silu(x):
    return x * jax.nn.sigmoid(x)


def _ada_kernel(c_ref, w_ref, b_ref, o_ref):
    a = _silu(c_ref[...]).astype(BF16)
    w = w_ref[0].astype(BF16)
    o_ref[0] = jnp.dot(a, w, preferred_element_type=F32) + b_ref[0]


def _ada_mod(c_all, w_ada, b_ada):
    depth, d, n = w_ada.shape
    rows = c_all.shape[0]
    tn = 512
    return pl.pallas_call(
        _ada_kernel,
        grid=(depth, n // tn),
        in_specs=[pl.BlockSpec((rows, d), lambda l, j: (0, 0)),
                  pl.BlockSpec((1, d, tn), lambda l, j: (l, 0, j)),
                  pl.BlockSpec((1, 1, tn), lambda l, j: (l, 0, j))],
        out_specs=pl.BlockSpec((1, rows, tn), lambda l, j: (l, 0, j)),
        out_shape=jax.ShapeDtypeStruct((depth, rows, n), F32),
        compiler_params=_params("arbitrary", "arbitrary"),
        name="ada_mod",
    )(c_all, w_ada, b_ada.reshape(depth, 1, n))


def _modnorm_kernel(x_ref, sh_ref, sc_ref, g_ref, o_ref):
    x = x_ref[...]
    ms = jnp.mean(x * x, axis=-1, keepdims=True)
    y = x * lax.rsqrt(ms + EPS) * g_ref[...]
    o_ref[...] = (y * (1.0 + sc_ref[...]) + sh_ref[...]).astype(BF16)


def _row_spec(arr, tm, width, col_block):
    if arr.shape[0] == 1:
        return pl.BlockSpec((1, width), lambda i, c=col_block: (0, c))
    return pl.BlockSpec((tm, width), lambda i, c=col_block: (i, c))


def _modnorm(x, mod, g):
    m, d = x.shape
    tm = min(512, m)
    return pl.pallas_call(
        _modnorm_kernel,
        grid=(m // tm,),
        in_specs=[pl.BlockSpec((tm, d), lambda i: (i, 0)),
                  _row_spec(mod, tm, d, 0),
                  _row_spec(mod, tm, d, 1),
                  pl.BlockSpec((1, d), lambda i: (0, 0))],
        out_specs=pl.BlockSpec((tm, d), lambda i: (i, 0)),
        out_shape=jax.ShapeDtypeStruct((m, d), BF16),
        compiler_params=_params("arbitrary"),
        name="modnorm",
    )(x, mod, mod, g.reshape(1, d))


def _linear(name, xs, w, col0, ncols, ncb, epilogue, extras, outs, tm=512):
    m = xs[0].shape[0]
    tm = min(tm, m)
    k_total = w.shape[0]
    nx, ne = len(xs), len(extras)

    def kernel(*refs):
        x_refs, w_ref = refs[:nx], refs[nx]
        e_refs, o_refs = refs[nx + 1:nx + 1 + ne], refs[nx + 1 + ne:]
        acc, k0 = None, 0
        for xr in x_refs:
            kp = xr.shape[1]
            part = jnp.dot(xr[...], w_ref[k0:k0 + kp, :], preferred_element_type=F32)
            acc = part if acc is None else acc + part
            k0 += kp
        epilogue(acc, e_refs, o_refs)

    in_specs = [pl.BlockSpec((tm, x.shape[1]), lambda n, i: (i, 0)) for x in xs]
    in_specs.append(pl.BlockSpec((k_total, ncols), lambda n, i: (0, col0 + n)))
    in_specs += [pl.BlockSpec(bs, im) for _, bs, im in extras]
    res = pl.pallas_call(
        kernel,
        grid=(ncb, m // tm),
        in_specs=in_specs,
        out_specs=[pl.BlockSpec(bs, im) for _, _, bs, im in outs],
        out_shape=[jax.ShapeDtypeStruct(s, dt) for s, dt, _, _ in outs],
        compiler_params=_params("arbitrary", "arbitrary"),
        name=name,
    )(*xs, w, *[a for a, _, _ in extras])
    return res


def _half_mean_matrix():
    r = lax.broadcasted_iota(jnp.int32, (HEAD_DIM, HEAD_DIM), 0) // QK_DIM
    c = lax.broadcasted_iota(jnp.int32, (HEAD_DIM, HEAD_DIM), 1) // QK_DIM
    return jnp.where(r == c, 1.0 / QK_DIM, 0.0).astype(BF16)


def _qk_norm(acc, g_row, s_mat):
    heads = acc.shape[1] // HEAD_DIM
    outs = []
    for h in range(heads):
        xh = acc[:, h * HEAD_DIM:(h + 1) * HEAD_DIM]
        t = xh * xh
        hi = t.astype(BF16)
        lo = (t - hi.astype(F32)).astype(BF16)
        ms = (jnp.dot(hi, s_mat, preferred_element_type=F32)
              + jnp.dot(lo, s_mat, preferred_element_type=F32))
        outs.append(xh * lax.rsqrt(ms + EPS))
    return jnp.concatenate(outs, axis=1) * g_row


ALIBI_LANES = 3


def _split_maps(y, aug_fn):
    tm = y.shape[0]
    lane = lax.broadcasted_iota(jnp.int32, (tm, HEAD_DIM), 1)
    first, second = [], []
    for h in range(y.shape[1] // HEAD_DIM):
        yh = y[:, h * HEAD_DIM:(h + 1) * HEAD_DIM]
        aug = aug_fn(h, lane)
        first.append(jnp.where(lane < QK_DIM, yh, aug))
        second.append(jnp.where(lane < QK_DIM, pltpu.roll(yh, QK_DIM, 1), aug))
    return jnp.concatenate(first, axis=1), jnp.concatenate(second, axis=1)


def _q_aug(h, lane):
    return jnp.where(lane < QK_DIM + ALIBI_LANES, 1.0, 0.0)


def _k_aug(h, lane, heads):
    slope = LOG2E * 2.0 ** (-8.0 * (h + 1) / heads)
    b = slope * lax.broadcasted_iota(jnp.int32, lane.shape, 0).astype(F32)
    hi = b.astype(BF16).astype(F32)
    mid = (b - hi).astype(BF16).astype(F32)
    lo = b - hi - mid
    return jnp.where(lane == QK_DIM, hi, jnp.where(lane == QK_DIM + 1, mid,
                                                   jnp.where(lane == QK_DIM + 2, lo, 0.0)))


def _even_in_proj(h, w_in, q_g, k_g, split_maps):
    m = h.shape[0]
    wdt = w_in.shape[1] // 6
    heads = wdt // HEAD_DIM
    tm = min(ATT_BLOCK, m)
    s_mat = _half_mean_matrix()
    row = lambda n, i: (i, 0)
    fix = lambda n, i: (0, 0)
    f32_out = ((m, wdt), F32, (tm, wdt), row)
    bf_out = ((m, wdt), BF16, (tm, wdt), row)
    qg = jnp.tile(q_g, heads).reshape(1, wdt)
    kg = jnp.tile(k_g, heads).reshape(1, wdt)
    norm_extras = lambda g: [(g, (1, wdt), fix), (s_mat, (HEAD_DIM, HEAD_DIM), fix)]

    def ep_plain(acc, e, o):
        o[0][...] = acc

    def ep_silu(acc, e, o):
        o[0][...] = _silu(acc)

    def ep_q(acc, e, o):
        y = _qk_norm(acc, e[0][...], e[1][...]) * (LOG2E * QK_DIM ** -0.5)
        if split_maps:
            y1, y2 = _split_maps(y, _q_aug)
            o[0][...] = y1.astype(BF16)
            o[1][...] = y2.astype(BF16)
        else:
            o[0][...] = y.astype(BF16)

    def ep_k(acc, e, o):
        y = _qk_norm(acc, e[0][...], e[1][...])
        o[0][...] = y
        if split_maps:
            y1, y2 = _split_maps(y, functools.partial(_k_aug, heads=heads))
            o[1][...] = y1.astype(BF16)
            o[2][...] = y2.astype(BF16)

    def ep_v(acc, e, o):
        o[0][...] = acc
        if split_maps:
            o[1][...] = acc.astype(BF16)

    maps = [bf_out, bf_out] if split_maps else []
    (u,) = _linear("in_even_u", [h], w_in, 0, wdt, 1, ep_plain, [], [f32_out], tm)
    (zs,) = _linear("in_even_zs", [h], w_in, 1, wdt, 1, ep_silu, [], [f32_out], tm)
    q = _linear("in_even_q", [h], w_in, 2, wdt, 1, ep_q, norm_extras(qg), maps or [bf_out], tm)
    k = _linear("in_even_k", [h], w_in, 3, wdt, 1, ep_k, norm_extras(kg), [f32_out] + maps, tm)
    v = _linear("in_even_v", [h], w_in, 4, wdt, 1, ep_v, [], [f32_out] + maps[:1], tm)
    (za,) = _linear("in_even_za", [h], w_in, 5, wdt, 1, ep_silu, [], [f32_out], tm)
    return u, zs, q, k, v, za


def _glu_proj(gy, w_glu, zs_gate):
    m, wdt = gy.shape
    tm = min(512, m)

    def ep(acc, e, o):
        o[0][...] = (acc[:, :wdt] * jax.nn.sigmoid(acc[:, wdt:]) * e[0][...]).astype(BF16)

    (out,) = _linear("s5_glu", [gy], w_glu, 0, 2 * wdt, 1, ep,
                     [(zs_gate, (tm, wdt), lambda n, i: (i, 0))],
                     [((m, wdt), BF16, (tm, wdt), lambda n, i: (i, 0))])
    return out


def _out_proj(name, xs, w, x_res, mod):
    m, d = x_res.shape
    tm = min(512, m)
    tn = min(1024, d)
    ncb = d // tn
    gate_col = 2 * ncb
    if mod.shape[0] == 1:
        gate = (mod, (1, tn), lambda n, i: (0, gate_col + n))
    else:
        gate = (mod, (tm, tn), lambda n, i: (i, gate_col + n))

    def ep(acc, e, o):
        o[0][...] = e[0][...] + e[1][...] * acc

    (y,) = _linear(name, xs, w, 0, tn, ncb, ep,
                   [(x_res, (tm, tn), lambda n, i: (i, n)), gate],
                   [((m, d), F32, (tm, tn), lambda n, i: (i, n))])
    return y


def _scan_tables(a_r, a_i, coef_s):
    n = a_r.shape[-1]
    row = lax.broadcasted_iota(jnp.int32, (SUBLANES, n), 0)
    a1 = (jnp.broadcast_to(a_r, (SUBLANES, n)), jnp.broadcast_to(a_i, (SUBLANES, n)))
    cmul = lambda x, y: (x[0] * y[0] - x[1] * y[1], x[0] * y[1] + x[1] * y[0])
    a2 = cmul(a1, a1)
    a4 = cmul(a2, a2)
    pc = a1
    for k in range(1, SUBLANES):
        nxt = cmul(pc, a1)
        pc = (jnp.where(row >= k, nxt[0], pc[0]), jnp.where(row >= k, nxt[1], pc[1]))
    for idx, (p, dist) in enumerate(((a1, 1), (a2, 2), (a4, 4))):
        coef_s[2 * idx] = jnp.where(row >= dist, p[0], 0.0)
        coef_s[2 * idx + 1] = jnp.where(row >= dist, p[1], 0.0)
    coef_s[6] = pc[0]
    coef_s[7] = pc[1]


def _scan_tile(a, bb, carry, coefs):
    m1r, m1i, m2r, m2i, m4r, m4i, pcr, pci = coefs
    cr, ci = carry
    for dist, mr, mi in ((1, m1r, m1i), (2, m2r, m2i), (4, m4r, m4i)):
        sa = pltpu.roll(a, dist, 0)
        sb = pltpu.roll(bb, dist, 0)
        a, bb = a + (mr * sa - mi * sb), bb + (mr * sb + mi * sa)
    return a + (pcr * cr - pci * ci), bb + (pcr * ci + pci * cr)


def _s5_block_kernel(u_ref, h0_ref, m_ref, w_ref, v_ref, d_ref, a8r_ref, a8i_ref, gy_ref, hs_ref,
                     x_s, y_s, st_s, coef_s):
    t = pl.program_id(1)
    rows = x_s.shape[0]
    ns = S5_TILE_STATES

    @pl.when(t == 0)
    def _():
        _scan_tables(a8r_ref[...], a8i_ref[...], coef_s)
        st_s[...] = h0_ref[0]

    u8 = jnp.concatenate([u_ref[pl.ds(s, rows, stride=SUBLANES), :] for s in range(SUBLANES)], axis=1)
    ub = u8.astype(BF16)
    x_s[...] = jnp.dot(ub, w_ref[0], preferred_element_type=F32)

    coefs = [coef_s[k] for k in range(8)]
    row = lax.broadcasted_iota(jnp.int32, (SUBLANES, ns), 0)

    def tile(ti, carry):
        r0 = pl.multiple_of(ti * SUBLANES, SUBLANES)
        a, bb = _scan_tile(x_s[pl.ds(r0, SUBLANES), :ns], x_s[pl.ds(r0, SUBLANES), ns:], carry, coefs)
        x_s[pl.ds(r0, SUBLANES), :ns] = jnp.where(row == 0, carry[0], pltpu.roll(a, 1, 0))
        x_s[pl.ds(r0, SUBLANES), ns:] = jnp.where(row == 0, carry[1], pltpu.roll(bb, 1, 0))
        return a[SUBLANES - 1:SUBLANES, :], bb[SUBLANES - 1:SUBLANES, :]

    cr, ci = lax.fori_loop(0, rows // SUBLANES, tile, (st_s[:, :ns], st_s[:, ns:]))
    st_s[:, :ns] = cr
    st_s[:, ns:] = ci

    y8 = (jnp.dot(ub, m_ref[0], preferred_element_type=F32)
          + jnp.dot(x_s[...].astype(BF16), v_ref[0], preferred_element_type=F32))
    for s in range(SUBLANES):
        ls_ = slice(s * LANES, (s + 1) * LANES)
        y_s[pl.ds(s, rows, stride=SUBLANES), :] = y8[:, ls_] + d_ref[...] * u8[:, ls_]
    gy_ref[...] = jax.nn.gelu(y_s[...]).astype(BF16)

    @pl.when(t == pl.num_programs(1) - 1)
    def _():
        hs_ref[0] = st_s[...]


def _s5_blocked(u, prm):
    lb, wdt = u.shape
    nc = wdt // LANES
    ns = S5_TILE_STATES
    rows = math.gcd(256, lb // SUBLANES)
    nt = lb // (rows * SUBLANES)
    tok = pl.BlockSpec((rows * SUBLANES, LANES), lambda c, t: (t, c))
    op = lambda a: pl.BlockSpec((1,) + a.shape[1:], lambda c, t: (c, 0, 0))
    vec = lambda width: pl.BlockSpec((1, width), lambda c, t: (0, c))
    st = pl.BlockSpec((1, 1, 2 * ns), lambda c, t: (c, 0, 0))
    gy, hs = pl.pallas_call(
        _s5_block_kernel,
        grid=(nc, nt),
        in_specs=[tok, st, op(prm["m"]), op(prm["w"]), op(prm["v"]), vec(LANES), vec(ns), vec(ns)],
        out_specs=[tok, st],
        out_shape=[jax.ShapeDtypeStruct((lb, wdt), BF16), jax.ShapeDtypeStruct((nc, 1, 2 * ns), F32)],
        scratch_shapes=[pltpu.VMEM((rows, 2 * ns), F32), pltpu.VMEM((rows * SUBLANES, LANES), F32),
                        pltpu.VMEM((1, 2 * ns), F32), pltpu.VMEM((8, SUBLANES, ns), F32)],
        compiler_params=_params("arbitrary", "arbitrary"),
        name="s5_blocked",
    )(u, jnp.zeros((nc, 1, 2 * ns), F32), prm["m"], prm["w"], prm["v"], prm["d"], prm["a8r"], prm["a8i"])
    hs = hs.reshape(nc, 2, ns)
    return gy, hs[:, 0].reshape(1, 1, nc * ns), hs[:, 1].reshape(1, 1, nc * ns)


def _s5_kernel(u_ref, h0r_ref, h0i_ref, wbr_ref, wbi_ref, wcr_ref, wci_ref, d_ref, ar_ref, ai_ref,
               gy_ref, hr_ref, hi_ref, xr_s, xi_s, sr_s, si_s, coef_s, *, lane_chunk):
    b, t = pl.program_id(0), pl.program_id(1)
    rows, ns = xr_s.shape
    nc = wbr_ref.shape[0]

    @pl.when((b == 0) & (t == 0))
    def _():
        _scan_tables(ar_ref[...], ai_ref[...], coef_s)

    @pl.when(t == 0)
    def _():
        sr_s[...] = h0r_ref[0]
        si_s[...] = h0i_ref[0]

    ub = u_ref[...].astype(BF16)
    for c in range(nc):
        uc = ub[:, c * LANES:(c + 1) * LANES]
        cs = slice(c * S5_TILE_STATES, (c + 1) * S5_TILE_STATES)
        xr_s[:, cs] = jnp.dot(uc, wbr_ref[c], preferred_element_type=F32)
        xi_s[:, cs] = jnp.dot(uc, wbi_ref[c], preferred_element_type=F32)

    for lc in range(ns // lane_chunk):
        ls = slice(lc * lane_chunk, (lc + 1) * lane_chunk)
        m1r, m1i, m2r, m2i, m4r, m4i, pcr, pci = [coef_s[k, :, ls] for k in range(8)]

        def tile(ti, carry, ls=ls, m1r=m1r, m1i=m1i, m2r=m2r, m2i=m2i, m4r=m4r, m4i=m4i,
                 pcr=pcr, pci=pci):
            cr, ci = carry
            r0 = pl.multiple_of(ti * SUBLANES, SUBLANES)
            a = xr_s[pl.ds(r0, SUBLANES), ls]
            bb = xi_s[pl.ds(r0, SUBLANES), ls]
            for dist, mr, mi in ((1, m1r, m1i), (2, m2r, m2i), (4, m4r, m4i)):
                sa = pltpu.roll(a, dist, 0)
                sb = pltpu.roll(bb, dist, 0)
                a, bb = a + (mr * sa - mi * sb), bb + (mr * sb + mi * sa)
            a, bb = a + (pcr * cr - pci * ci), bb + (pcr * ci + pci * cr)
            xr_s[pl.ds(r0, SUBLANES), ls] = a
            xi_s[pl.ds(r0, SUBLANES), ls] = bb
            return a[SUBLANES - 1:SUBLANES, :], bb[SUBLANES - 1:SUBLANES, :]

        cr, ci = lax.fori_loop(0, rows // SUBLANES, tile, (sr_s[:, ls], si_s[:, ls]))
        sr_s[:, ls] = cr
        si_s[:, ls] = ci

    for c in range(nc):
        cs = slice(c * S5_TILE_STATES, (c + 1) * S5_TILE_STATES)
        os_ = slice(c * LANES, (c + 1) * LANES)
        y = (jnp.dot(xr_s[:, cs].astype(BF16), wcr_ref[c], preferred_element_type=F32)
             - jnp.dot(xi_s[:, cs].astype(BF16), wci_ref[c], preferred_element_type=F32)
             + d_ref[:, os_] * u_ref[:, os_])
        gy_ref[:, os_] = jax.nn.gelu(y).astype(BF16)

    @pl.when(t == pl.num_programs(1) - 1)
    def _():
        hr_ref[0] = sr_s[...]
        hi_ref[0] = si_s[...]


def _s5(u, h0r, h0i, prm, nb, lb):
    wdt = u.shape[1]
    ns = h0r.shape[-1]
    nc = wdt // LANES
    rows = min(256, lb)
    nt = lb // rows
    full = lambda a: pl.BlockSpec(a.shape, lambda b, t, nd=a.ndim: (0,) * nd)
    st = pl.BlockSpec((1, 1, ns), lambda b, t: (b, 0, 0))
    tok = pl.BlockSpec((rows, wdt), lambda b, t: (b * nt + t, 0))
    return pl.pallas_call(
        functools.partial(_s5_kernel, lane_chunk=min(512, ns)),
        grid=(nb, nt),
        in_specs=[tok, st, st] + [full(prm[k]) for k in ("wbr", "wbi", "wcr", "wci", "d", "ar", "ai")],
        out_specs=[tok, st, st],
        out_shape=[jax.ShapeDtypeStruct((nb * lb, wdt), BF16),
                   jax.ShapeDtypeStruct((nb, 1, ns), F32),
                   jax.ShapeDtypeStruct((nb, 1, ns), F32)],
        scratch_shapes=[pltpu.VMEM((rows, ns), F32), pltpu.VMEM((rows, ns), F32),
                        pltpu.VMEM((1, ns), F32), pltpu.VMEM((1, ns), F32),
                        pltpu.VMEM((8, SUBLANES, ns), F32)],
        compiler_params=_params("arbitrary", "arbitrary"),
        name="s5_scan",
    )(u, h0r, h0i, prm["wbr"], prm["wbi"], prm["wcr"], prm["wci"], prm["d"], prm["ar"], prm["ai"])


def _s5_params(lam_re, lam_im, log_dt, b_re, b_im, c_re, c_im, d_skip):
    g, p = lam_re.shape
    nc = g // S5_LANE_GROUPS
    dt = jnp.exp(log_dt)[:, None]
    mag = jnp.exp(lam_re * dt)
    ar = mag * jnp.cos(lam_im * dt)
    ai = mag * jnp.sin(lam_im * dt)
    den = lam_re * lam_re + lam_im * lam_im
    zr = ((ar - 1.0) * lam_re + ai * lam_im) / den
    zi = (ai * lam_re - (ar - 1.0) * lam_im) / den
    bbr = zr[..., None] * b_re - zi[..., None] * b_im
    bbi = zr[..., None] * b_im + zi[..., None] * b_re
    eye = jnp.eye(S5_LANE_GROUPS, dtype=F32)

    def bdiag_in(x):
        t = x.reshape(nc, S5_LANE_GROUPS, p, S5_GROUP).transpose(0, 1, 3, 2)
        return (t[:, :, :, None, :] * eye[None, :, None, :, None]).reshape(nc, LANES, S5_TILE_STATES).astype(BF16)

    def bdiag_out(x):
        t = x.reshape(nc, S5_LANE_GROUPS, S5_GROUP, p).transpose(0, 1, 3, 2)
        return (t[:, :, :, None, :] * eye[None, :, None, :, None]).reshape(nc, S5_TILE_STATES, LANES).astype(BF16)

    prm = dict(wbr=bdiag_in(bbr), wbi=bdiag_in(bbi), wcr=bdiag_out(c_re), wci=bdiag_out(c_im),
               d=d_skip.reshape(1, g * S5_GROUP), ar=ar.reshape(1, g * p), ai=ai.reshape(1, g * p))
    prm.update(_s5_block_operators(ar, ai, bbr, bbi, c_re, c_im))
    return prm


def _s5_block_operators(ar, ai, bbr, bbi, c_re, c_im):
    g, p = ar.shape
    nc = g // S5_LANE_GROUPS
    t8 = SUBLANES
    hp = lax.Precision.HIGHEST
    pr, pi = [jnp.ones_like(ar)], [jnp.zeros_like(ai)]
    for _ in range(t8):
        pr, pi = pr + [pr[-1] * ar - pi[-1] * ai], pi + [pr[-1] * ai + pi[-1] * ar]
    pw_r, pw_i = jnp.stack(pr), jnp.stack(pi)

    def c_times(lo):
        a_r, a_i = pw_r[lo:lo + t8, :, None, :], pw_i[lo:lo + t8, :, None, :]
        return c_re[None] * a_r - c_im[None] * a_i, c_re[None] * a_i + c_im[None] * a_r

    eye = jnp.eye(S5_LANE_GROUPS, dtype=F32)

    e_r, e_i = c_times(0)
    k = (jnp.einsum("jgop,gpi->jgoi", e_r, bbr, precision=hp)
         - jnp.einsum("jgop,gpi->jgoi", e_i, bbi, precision=hp))
    zero = jnp.zeros_like(k[0])
    toe = jnp.stack([jnp.stack([k[t - s] if t >= s else zero for t in range(t8)])
                     for s in range(t8)])
    toe = toe.reshape(t8, t8, nc, S5_LANE_GROUPS, S5_GROUP, S5_GROUP)
    m = jnp.einsum("stcioh,ij->csihtjo", toe, eye, precision=hp)
    m = m.reshape(nc, t8 * LANES, t8 * LANES)

    rev_r, rev_i = jnp.stack(pr[t8 - 1::-1]), jnp.stack(pi[t8 - 1::-1])
    ab_r = rev_r[..., None] * bbr[None] - rev_i[..., None] * bbi[None]
    ab_i = rev_r[..., None] * bbi[None] + rev_i[..., None] * bbr[None]
    ab = jnp.stack([ab_r, ab_i]).reshape(2, t8, nc, S5_LANE_GROUPS, p, S5_GROUP)
    w = jnp.einsum("rsciph,ij->csihrjp", ab, eye, precision=hp)
    w = w.reshape(nc, t8 * LANES, 2 * S5_TILE_STATES)

    f_r, f_i = c_times(1)
    cf = jnp.stack([f_r, -f_i]).reshape(2, t8, nc, S5_LANE_GROUPS, S5_GROUP, p)
    v = jnp.einsum("rtciop,ij->criptjo", cf, eye, precision=hp)
    v = v.reshape(nc, 2 * S5_TILE_STATES, t8 * LANES)

    return dict(m=m.astype(BF16), w=w.astype(BF16), v=v.astype(BF16),
                a8r=pw_r[t8].reshape(1, g * p), a8i=pw_i[t8].reshape(1, g * p))


def _stack_maps(q):
    lane = lax.broadcasted_iota(jnp.int32, q.shape, 1)
    zero = jnp.zeros_like(q)
    return jnp.concatenate([jnp.where(lane < QK_DIM, q, zero), jnp.where(lane >= QK_DIM, q, zero)], axis=0)


def _qk(qq, kb):
    return lax.dot_general(qq, kb, (((1,), (1,)), ((), ())), preferred_element_type=F32)


def _attn_finish(o, g_row, gate, lam_init):
    ms = jnp.mean(o * o, axis=-1, keepdims=True)
    return (o * lax.rsqrt(ms + EPS) * g_row * (1.0 - lam_init) * gate).astype(BF16)


def _attn_kernel(slopes_ref, lam_ref, q1_ref, q2_ref, k1_ref, k2_ref, vt_ref, za_ref, og_ref, o_ref,
                 m_s, l_s, acc_s, sa_s, sb_s, ma_s, mb_s, dbias_s, *, lam_init):
    h, i = pl.program_id(0), pl.program_id(1)
    t = q1_ref.shape[0]
    nq = 2 * t
    slope = slopes_ref[h]

    @pl.when(i == 0)
    def _():
        k_local = lax.broadcasted_iota(jnp.int32, (t, nq), 0)
        q_col = lax.broadcasted_iota(jnp.int32, (t, nq), 1)
        q_local = jnp.where(q_col >= t, q_col - t, q_col)
        vis = (k_local // CHUNK) <= (q_local // CHUNK)
        dist = jnp.abs(q_local - k_local).astype(F32)
        dbias_s[...] = (jnp.where(vis, -slope * dist, NEG_INF)
                        + slope * (q_local - k_local).astype(F32))

    q1 = q1_ref[...]
    q2 = q2_ref[...]
    m_s[...] = jnp.full(m_s.shape, -jnp.inf, F32)
    l_s[...] = jnp.zeros(l_s.shape, F32)
    acc_s[...] = jnp.zeros(acc_s.shape, F32)

    def scores(blk, diagonal, s_ref, mx_ref):
        r0 = pl.multiple_of(blk * t, t)
        for half, (k_ref, q) in enumerate(((k1_ref, q1), (k2_ref, q2))):
            cols = slice(half * t, (half + 1) * t)
            s_t = _qk(k_ref[pl.ds(r0, t), :], q)
            if diagonal:
                s_t = s_t + dbias_s[:, cols]
            s_ref[:, cols] = s_t
            mx_ref[:, cols] = jnp.max(s_t, axis=0, keepdims=True)

    def consume(blk, s_ref, mx_ref):
        r0 = pl.multiple_of(blk * t, t)
        shift = slope * ((blk - i) * t).astype(F32)
        m_old = m_s[...]
        m_new = jnp.maximum(m_old, mx_ref[...] + shift)
        alpha = jnp.exp2(m_old - m_new)
        p = jnp.exp2(s_ref[...] - (m_new - shift))
        l_s[...] = alpha * l_s[...] + jnp.sum(p, axis=0, keepdims=True)
        acc_s[...] = alpha * acc_s[...] + jnp.dot(vt_ref[:, pl.ds(r0, t)], p.astype(BF16),
                                                  preferred_element_type=F32)
        m_s[...] = m_new

    scores(i, True, sa_s, ma_s)
    pairs = i // 2

    def pair(jj, carry):
        j0 = 2 * jj
        scores(j0, False, sb_s, mb_s)
        consume(jnp.where(jj == 0, i, j0 - 1), sa_s, ma_s)
        scores(j0 + 1, False, sa_s, ma_s)
        consume(j0, sb_s, mb_s)
        return carry

    lax.fori_loop(0, pairs, pair, 0)
    in_a = jnp.where(pairs == 0, i, 2 * pairs - 1)

    @pl.when(i % 2 == 1)
    def _():
        scores(i - 1, False, sb_s, mb_s)
        consume(in_a, sa_s, ma_s)
        consume(i - 1, sb_s, mb_s)

    @pl.when(i % 2 == 0)
    def _():
        consume(in_a, sa_s, ma_s)

    acc = acc_s[...]
    l = l_s[...]
    o_t = acc[:, :t] / l[:, :t] - lam_ref[0] * (acc[:, t:] / l[:, t:])
    ms = jnp.mean(o_t * o_t, axis=0, keepdims=True)
    y = (o_t * lax.rsqrt(ms + EPS)).T
    o_ref[...] = (y * og_ref[...] * (1.0 - lam_init) * za_ref[...]).astype(BF16)


def _attention_prompt(q1, q2, k1, k2, vb, za_gate, out_g, lam, lam_init):
    m, wdt = q1.shape
    heads = wdt // HEAD_DIM
    t = min(ATT_BLOCK, m)
    slopes = LOG2E * 2.0 ** (-8.0 * jnp.arange(1, heads + 1, dtype=F32) / heads)
    smem = pl.BlockSpec(memory_space=pltpu.SMEM)
    blk = pl.BlockSpec((t, HEAD_DIM), lambda h, i: (i, h))
    keys = pl.BlockSpec((m, HEAD_DIM), lambda h, i: (0, h))
    return pl.pallas_call(
        functools.partial(_attn_kernel, lam_init=lam_init),
        grid=(heads, m // t),
        in_specs=[smem, smem, blk, blk, keys, keys,
                  pl.BlockSpec((HEAD_DIM, m), lambda h, i: (h, 0)),
                  blk, pl.BlockSpec((1, HEAD_DIM), lambda h, i: (0, 0))],
        out_specs=blk,
        out_shape=jax.ShapeDtypeStruct((m, wdt), BF16),
        scratch_shapes=[pltpu.VMEM((1, 2 * t), F32), pltpu.VMEM((1, 2 * t), F32),
                        pltpu.VMEM((HEAD_DIM, 2 * t), F32),
                        pltpu.VMEM((t, 2 * t), F32), pltpu.VMEM((t, 2 * t), F32),
                        pltpu.VMEM((1, 2 * t), F32), pltpu.VMEM((1, 2 * t), F32),
                        pltpu.VMEM((t, 2 * t), F32)],
        compiler_params=_params("arbitrary", "arbitrary"),
        name="attn_prompt",
    )(slopes, lam, q1, q2, k1, k2, vb.T, za_gate, out_g.reshape(1, HEAD_DIM))


MASKED_DIST = 1e32


def _attn_cached_kernel(lam_ref, q_ref, kn_ref, vn_ref, kc_ref, vc_ref, za_ref, og_ref, o_ref,
                        dc_s, dn_s, *, lam_init, heads):
    ls = q_ref.shape[0]
    past = kc_ref.shape[2]
    lam = lam_ref[0]

    @pl.when(pl.program_id(0) == 0)
    def _():
        def table(n, k0):
            q_row = lax.broadcasted_iota(jnp.int32, (2 * ls, n), 0)
            q_pos = past + jnp.where(q_row >= ls, q_row - ls, q_row)
            k_pos = k0 + lax.broadcasted_iota(jnp.int32, (2 * ls, n), 1)
            vis = (k_pos // CHUNK) <= (q_pos // CHUNK)
            return jnp.where(vis, jnp.abs(q_pos - k_pos).astype(F32), MASKED_DIST)

        dc_s[...] = table(past, 0)
        dn_s[...] = table(ls, past)

    for h in range(heads):
        hs = slice(h * HEAD_DIM, (h + 1) * HEAD_DIM)
        slope = LOG2E * 2.0 ** (-8.0 * (h + 1) / heads)
        qq = _stack_maps(q_ref[:, hs])
        sc = _qk(qq, kc_ref[0, h]) - slope * dc_s[...]
        sn = _qk(qq, kn_ref[:, hs].astype(BF16)) - slope * dn_s[...]
        mx = jnp.maximum(jnp.max(sc, axis=1, keepdims=True), jnp.max(sn, axis=1, keepdims=True))
        pc = jnp.exp2(sc - mx)
        pn = jnp.exp2(sn - mx)
        inv = 1.0 / (jnp.sum(pc, axis=1, keepdims=True) + jnp.sum(pn, axis=1, keepdims=True))
        pc = pc * inv
        pn = pn * inv
        wc = (pc[:ls] - lam * pc[ls:]).astype(BF16)
        wn = (pn[:ls] - lam * pn[ls:]).astype(BF16)
        o = (jnp.dot(wc, vc_ref[0, h], preferred_element_type=F32)
             + jnp.dot(wn, vn_ref[:, hs].astype(BF16), preferred_element_type=F32))
        o_ref[:, hs] = _attn_finish(o, og_ref[...], za_ref[:, hs], lam_init)


def _attention_cached(q, k_new, v_new, cache_k, cache_v, za_gate, out_g, lam, lam_init, nb, ls):
    wdt = q.shape[1]
    heads = wdt // HEAD_DIM
    past = cache_k.shape[1]
    head_major = lambda a: jnp.transpose(a.astype(BF16), (0, 2, 1, 3))
    tok = pl.BlockSpec((ls, wdt), lambda b: (b, 0))
    cache = pl.BlockSpec((1, heads, past, HEAD_DIM), lambda b: (b, 0, 0, 0))
    return pl.pallas_call(
        functools.partial(_attn_cached_kernel, lam_init=lam_init, heads=heads),
        grid=(nb,),
        in_specs=[pl.BlockSpec(memory_space=pltpu.SMEM), tok, tok, tok, cache, cache, tok,
                  pl.BlockSpec((1, HEAD_DIM), lambda b: (0, 0))],
        out_specs=tok,
        out_shape=jax.ShapeDtypeStruct((nb * ls, wdt), BF16),
        scratch_shapes=[pltpu.VMEM((2 * ls, past), F32), pltpu.VMEM((2 * ls, ls), F32)],
        compiler_params=_params("arbitrary"),
        name="attn_cached",
    )(lam, q, k_new, v_new, head_major(cache_k), head_major(cache_v), za_gate,
      out_g.reshape(1, HEAD_DIM))


def _odd_in_proj(h, w_in):
    m, d = h.shape
    tm = min(512, m)
    tn = min(512, d)
    ncb = d // tn

    def kernel(h_ref, wa_ref, wb_ref, wz_ref, g_ref, z_ref):
        x = h_ref[...]
        a = jnp.dot(x, wa_ref[...], preferred_element_type=F32)
        b = jnp.dot(x, wb_ref[...], preferred_element_type=F32)
        z = jnp.dot(x, wz_ref[...], preferred_element_type=F32)
        g_ref[...] = a * jax.nn.sigmoid(b)
        z_ref[...] = _silu(z)

    wspec = lambda off: pl.BlockSpec((d, tn), lambda n, i, off=off: (0, n + off))
    out = pl.BlockSpec((tm, tn), lambda n, i: (i, n))
    return pl.pallas_call(
        kernel,
        grid=(ncb, m // tm),
        in_specs=[pl.BlockSpec((tm, d), lambda n, i: (i, 0)), wspec(0), wspec(ncb), wspec(2 * ncb)],
        out_specs=[out, out],
        out_shape=[jax.ShapeDtypeStruct((m, d), F32), jax.ShapeDtypeStruct((m, d), F32)],
        compiler_params=_params("arbitrary", "arbitrary"),
        name="in_odd",
    )(h, w_in, w_in, w_in)


CONV_HALO = 32


LN_ROWS = 64


def _conv_kernel(cur_ref, prev_ref, ctx_ref, zg_ref, w_ref, b_ref, lg_ref, lb_ref, o_ref, xp_s, y_s,
                 sh_s, *, row_chunk):
    t = pl.program_id(1)
    tm, c = cur_ref.shape

    @pl.when(t == 0)
    def _():
        xp_s[0:CONV_HALO, :] = ctx_ref[0]

    @pl.when(t > 0)
    def _():
        xp_s[0:CONV_HALO, :] = prev_ref[...]

    xp_s[CONV_HALO:CONV_HALO + tm, :] = cur_ref[...]
    xp_s[CONV_HALO + tm:, :] = jnp.zeros((SUBLANES, c), F32)

    first = CONV_HALO - (CONV_W - 1)
    span = row_chunk + CONV_HALO

    def lane_body(lc, carry):
        c0 = pl.multiple_of(lc * LANES, LANES)

        def row_body(rc, carry2):
            r0 = pl.multiple_of(rc * row_chunk, row_chunk)
            blk = xp_s[pl.ds(r0, span + SUBLANES), pl.ds(c0, LANES)]
            for phase in range(SUBLANES):
                sh_s[phase] = blk[phase:phase + span]
            acc = jnp.zeros((row_chunk, LANES), F32)
            for phase in range(SUBLANES):
                for a in range(span // SUBLANES):
                    tap = SUBLANES * a + phase - first
                    if 0 <= tap < CONV_W:
                        acc = acc + (w_ref[pl.ds(tap, 1), pl.ds(c0, LANES)]
                                     * sh_s[phase, SUBLANES * a:SUBLANES * a + row_chunk, :])
            y_s[pl.ds(r0, row_chunk), pl.ds(c0, LANES)] = acc
            return carry2

        return lax.fori_loop(0, tm // row_chunk, row_body, carry)

    lax.fori_loop(0, c // LANES, lane_body, 0)

    ln_rows = min(LN_ROWS, tm)

    def ln_body(r, carry):
        rows = pl.ds(pl.multiple_of(r * ln_rows, ln_rows), ln_rows)
        y = y_s[rows, :] + b_ref[...]
        mu = jnp.mean(y, axis=-1, keepdims=True)
        yc = y - mu
        var = jnp.mean(yc * yc, axis=-1, keepdims=True)
        yn = yc * lax.rsqrt(var + EPS) * lg_ref[...] + lb_ref[...]
        o_ref[rows, :] = (_silu(yn) * zg_ref[rows, :]).astype(BF16)
        return carry

    lax.fori_loop(0, tm // ln_rows, ln_body, 0)


def _conv_module(g, ctx, zg, conv_w, conv_b, ln_g, ln_b, nb, lb):
    c = g.shape[1]
    tm = min(256, lb)
    nt = lb // tm
    halo_blocks = tm // CONV_HALO
    w_pad = jnp.zeros((CONV_HALO, c), F32).at[:CONV_W].set(conv_w)
    cur = pl.BlockSpec((tm, c), lambda b, t: (b * nt + t, 0))
    prev = pl.BlockSpec((CONV_HALO, c), lambda b, t: (jnp.maximum((b * nt + t) * halo_blocks - 1, 0), 0))
    vec = pl.BlockSpec((1, c), lambda b, t: (0, 0))
    return pl.pallas_call(
        functools.partial(_conv_kernel, row_chunk=min(64, tm)),
        grid=(nb, nt),
        in_specs=[cur, prev, pl.BlockSpec((1, CONV_HALO, c), lambda b, t: (b, 0, 0)), cur,
                  pl.BlockSpec((CONV_HALO, c), lambda b, t: (0, 0)), vec, vec, vec],
        out_specs=cur,
        out_shape=jax.ShapeDtypeStruct((nb * lb, c), BF16),
        scratch_shapes=[pltpu.VMEM((CONV_HALO + tm + SUBLANES, c), F32), pltpu.VMEM((tm, c), F32),
                        pltpu.VMEM((SUBLANES, min(64, tm) + CONV_HALO, LANES), F32)],
        compiler_params=_params("arbitrary", "arbitrary"),
        name="conv_module",
    )(g, g, ctx, zg, w_pad, conv_b.reshape(1, c), ln_g.reshape(1, c), ln_b.reshape(1, c))


def kernel(x_prompt, x_sample, c_prompt, c_sample, cache_k, cache_v, state_s5_re, state_s5_im, state_conv, norm_g, w_ada, b_ada, w_in_even, w_out_even, s5_lam_re, s5_lam_im, s5_log_dt, s5_b_re, s5_b_im, s5_c_re, s5_c_im, s5_d, s5_w_glu, q_norm_g, k_norm_g, lam_q1, lam_k1, lam_q2, lam_k2, attn_out_g, w_in_odd, conv_w, conv_b, conv_ln_g, conv_ln_b, w_out_odd):
    bp, lp, d = x_prompt.shape
    bs, ls, _ = x_sample.shape
    assert bp == 1 and w_ada.shape[0] == 2, "one prompt sequence, one even and one odd layer"
    wdt = d // 2
    heads = wdt // HEAD_DIM
    ns = (wdt // S5_GROUP) * S5_STATE
    ms = bs * ls

    xp = x_prompt.reshape(lp, d)
    xs = x_sample.reshape(ms, d)

    rows = bp + bs
    rows_pad = -(-rows // 16) * 16
    c_all = jnp.concatenate([c_prompt, c_sample, jnp.zeros((rows_pad - rows, d), F32)], axis=0)
    mod = _ada_mod(c_all, w_ada, b_ada)
    mod_p = [mod[l, :bp] for l in range(2)]
    mod_s = [jnp.repeat(mod[l, bp:rows], ls, axis=0) for l in range(2)]

    w_in_e = w_in_even[0].astype(BF16)
    w_out_e = w_out_even[0].astype(BF16)
    w_glu = s5_w_glu[0].astype(BF16)
    w_in_o = w_in_odd[0].astype(BF16)
    w_out_o = w_out_odd[0].astype(BF16)

    s5p = _s5_params(s5_lam_re[0], s5_lam_im[0], s5_log_dt[0], s5_b_re[0], s5_b_im[0],
                     s5_c_re[0], s5_c_im[0], s5_d[0])
    lam_init = 0.8 - 0.6 * math.exp(-0.3 * 0)
    lam = (jnp.exp(jnp.sum(lam_q1[0] * lam_k1[0])) - jnp.exp(jnp.sum(lam_q2[0] * lam_k2[0]))
           + lam_init).reshape(1).astype(F32)

    def even_layer(x, modl, nb, lb, h0r, h0i, cache):
        h = _modnorm(x, modl, norm_g[0])
        u, zs, q, k, v, za = _even_in_proj(h, w_in_e, q_norm_g[0], k_norm_g[0], cache is None)
        gy, hr, hi = _s5_blocked(u, s5p) if cache is None else _s5(u, h0r, h0i, s5p, nb, lb)
        s5_out = _glu_proj(gy, w_glu, zs)
        if cache is None:
            att = _attention_prompt(q[0], q[1], k[1], k[2], v[1], za, attn_out_g[0], lam, lam_init)
        else:
            att = _attention_cached(q[0], k[0], v[0], cache[0], cache[1], za, attn_out_g[0], lam,
                                    lam_init, nb, lb)
        y = _out_proj("out_even", [s5_out, att], w_out_e, x, modl)
        return y, k[0], v[0], hr, hi

    def odd_layer(x, modl, nb, lb, ctx):
        h = _modnorm(x, modl, norm_g[1])
        g, zg = _odd_in_proj(h, w_in_o)
        yc = _conv_module(g, ctx, zg, conv_w[0], conv_b[0], conv_ln_g[0], conv_ln_b[0], nb, lb)
        y = _out_proj("out_odd", [yc], w_out_o, x, modl)
        return y, g

    zero_state = jnp.zeros((bp, 1, ns), F32)
    yp, kp, vp, hrp, hip = even_layer(xp, mod_p[0], bp, lp, zero_state, zero_state, None)
    past = cache_k.shape[2]
    ys, ksn, vsn, hrs, his = even_layer(
        xs, mod_s[0], bs, ls, state_s5_re[0].reshape(bs, 1, ns), state_s5_im[0].reshape(bs, 1, ns),
        (cache_k[0], cache_v[0]))

    pad = CONV_HALO - (CONV_W - 1)
    ctx_p = jnp.zeros((bp, CONV_HALO, d), F32)
    ctx_s = jnp.pad(state_conv[0], ((0, 0), (pad, 0), (0, 0)))
    yp, gp = odd_layer(yp, mod_p[1], bp, lp, ctx_p)
    ys, gs = odd_layer(ys, mod_s[1], bs, ls, ctx_s)

    groups = wdt // S5_GROUP
    tail = CONV_W - 1
    conv_p = gp.reshape(bp, lp, d)[:, lp - tail:]
    conv_s = jnp.concatenate([state_conv[0], gs.reshape(bs, ls, d)], axis=1)[:, -tail:]
    return (yp.reshape(bp, lp, d), ys.reshape(bs, ls, d),
            kp.reshape(1, bp, lp, heads, HEAD_DIM), vp.reshape(1, bp, lp, heads, HEAD_DIM),
            hrp.reshape(1, bp, groups, S5_STATE), hip.reshape(1, bp, groups, S5_STATE), conv_p[None],
            ksn.reshape(1, bs, ls, heads, HEAD_DIM), vsn.reshape(1, bs, ls, heads, HEAD_DIM),
            hrs.reshape(1, bs, groups, S5_STATE), his.reshape(1, bs, groups, S5_STATE), conv_s[None])
```

```python
import functools
import math

import jax
import jax.numpy as jnp
from jax import lax
from jax.experimental import pallas as pl
from jax.experimental.pallas import tpu as pltpu

F32 = jnp.float32
BF16 = jnp.bfloat16

CHUNK = 64
HEAD_DIM = 128
QK_DIM = 64
S5_GROUP = 16
S5_STATE = 64
CONV_W = 31
EPS = 1e-6
NEG_INF = -1e30
LOG2E = 1.4426950408889634

LANES = 128
SUBLANES = 8
ATT_BLOCK = 512
S5_LANE_GROUPS = LANES // S5_GROUP
S5_TILE_STATES = S5_LANE_GROUPS * S5_STATE
VMEM_LIMIT = 48 * 1024 * 1024


def _params(*sem):
    return pltpu.CompilerParams(dimension_semantics=sem, vmem_limit_bytes=VMEM_LIMIT)


def _silu(x):
    return x * jax.nn.sigmoid(x)


def _ada_kernel(c_ref, w_ref, b_ref, o_ref):
    a = _silu(c_ref[...]).astype(BF16)
    w = w_ref[0].astype(BF16)
    o_ref[0] = jnp.dot(a, w, preferred_element_type=F32) + b_ref[0]


def _ada_mod(c_all, w_ada, b_ada):
    depth, d, n = w_ada.shape
    rows = c_all.shape[0]
    tn = 512
    return pl.pallas_call(
        _ada_kernel,
        grid=(depth, n // tn),
        in_specs=[pl.BlockSpec((rows, d), lambda l, j: (0, 0)),
                  pl.BlockSpec((1, d, tn), lambda l, j: (l, 0, j)),
                  pl.BlockSpec((1, 1, tn), lambda l, j: (l, 0, j))],
        out_specs=pl.BlockSpec((1, rows, tn), lambda l, j: (l, 0, j)),
        out_shape=jax.ShapeDtypeStruct((depth, rows, n), F32),
        compiler_params=_params("arbitrary", "arbitrary"),
        name="ada_mod",
    )(c_all, w_ada, b_ada.reshape(depth, 1, n))


def _modnorm_kernel(x_ref, sh_ref, sc_ref, g_ref, o_ref):
    x = x_ref[...]
    ms = jnp.mean(x * x, axis=-1, keepdims=True)
    y = x * lax.rsqrt(ms + EPS) * g_ref[...]
    o_ref[...] = (y * (1.0 + sc_ref[...]) + sh_ref[...]).astype(BF16)


def _row_spec(arr, tm, width, col_block):
    if arr.shape[0] == 1:
        return pl.BlockSpec((1, width), lambda i, c=col_block: (0, c))
    return pl.BlockSpec((tm, width), lambda i, c=col_block: (i, c))


def _modnorm(x, mod, g):
    m, d = x.shape
    tm = min(512, m)
    return pl.pallas_call(
        _modnorm_kernel,
        grid=(m // tm,),
        in_specs=[pl.BlockSpec((tm, d), lambda i: (i, 0)),
                  _row_spec(mod, tm, d, 0),
                  _row_spec(mod, tm, d, 1),
                  pl.BlockSpec((1, d), lambda i: (0, 0))],
        out_specs=pl.BlockSpec((tm, d), lambda i: (i, 0)),
        out_shape=jax.ShapeDtypeStruct((m, d), BF16),
        compiler_params=_params("arbitrary"),
        name="modnorm",
    )(x, mod, mod, g.reshape(1, d))


def _linear(name, xs, w, col0, ncols, ncb, epilogue, extras, outs, tm=512):
    m = xs[0].shape[0]
    tm = min(tm, m)
    k_total = w.shape[0]
    nx, ne = len(xs), len(extras)

    def kernel(*refs):
        x_refs, w_ref = refs[:nx], refs[nx]
        e_refs, o_refs = refs[nx + 1:nx + 1 + ne], refs[nx + 1 + ne:]
        acc, k0 = None, 0
        for xr in x_refs:
            kp = xr.shape[1]
            part = jnp.dot(xr[...], w_ref[k0:k0 + kp, :], preferred_element_type=F32)
            acc = part if acc is None else acc + part
            k0 += kp
        epilogue(acc, e_refs, o_refs)

    in_specs = [pl.BlockSpec((tm, x.shape[1]), lambda n, i: (i, 0)) for x in xs]
    in_specs.append(pl.BlockSpec((k_total, ncols), lambda n, i: (0, col0 + n)))
    in_specs += [pl.BlockSpec(bs, im) for _, bs, im in extras]
    res = pl.pallas_call(
        kernel,
        grid=(ncb, m // tm),
        in_specs=in_specs,
        out_specs=[pl.BlockSpec(bs, im) for _, _, bs, im in outs],
        out_shape=[jax.ShapeDtypeStruct(s, dt) for s, dt, _, _ in outs],
        compiler_params=_params("arbitrary", "arbitrary"),
        name=name,
    )(*xs, w, *[a for a, _, _ in extras])
    return res


def _half_mean_matrix():
    r = lax.broadcasted_iota(jnp.int32, (HEAD_DIM, HEAD_DIM), 0) // QK_DIM
    c = lax.broadcasted_iota(jnp.int32, (HEAD_DIM, HEAD_DIM), 1) // QK_DIM
    return jnp.where(r == c, 1.0 / QK_DIM, 0.0).astype(BF16)


def _qk_norm(acc, g_row, s_mat):
    heads = acc.shape[1] // HEAD_DIM
    outs = []
    for h in range(heads):
        xh = acc[:, h * HEAD_DIM:(h + 1) * HEAD_DIM]
        t = xh * xh
        hi = t.astype(BF16)
        lo = (t - hi.astype(F32)).astype(BF16)
        ms = (jnp.dot(hi, s_mat, preferred_element_type=F32)
              + jnp.dot(lo, s_mat, preferred_element_type=F32))
        outs.append(xh * lax.rsqrt(ms + EPS))
    return jnp.concatenate(outs, axis=1) * g_row


ALIBI_LANES = 3


def _split_maps(y, aug_fn):
    tm = y.shape[0]
    lane = lax.broadcasted_iota(jnp.int32, (tm, HEAD_DIM), 1)
    first, second = [], []
    for h in range(y.shape[1] // HEAD_DIM):
        yh = y[:, h * HEAD_DIM:(h + 1) * HEAD_DIM]
        aug = aug_fn(h, lane)
        first.append(jnp.where(lane < QK_DIM, yh, aug))
        second.append(jnp.where(lane < QK_DIM, pltpu.roll(yh, QK_DIM, 1), aug))
    return jnp.concatenate(first, axis=1), jnp.concatenate(second, axis=1)


def _q_aug(h, lane):
    return jnp.where(lane < QK_DIM + ALIBI_LANES, 1.0, 0.0)


def _k_aug(h, lane, heads):
    slope = LOG2E * 2.0 ** (-8.0 * (h + 1) / heads)
    b = slope * lax.broadcasted_iota(jnp.int32, lane.shape, 0).astype(F32)
    hi = b.astype(BF16).astype(F32)
    mid = (b - hi).astype(BF16).astype(F32)
    lo = b - hi - mid
    return jnp.where(lane == QK_DIM, hi, jnp.where(lane == QK_DIM + 1, mid,
                                                   jnp.where(lane == QK_DIM + 2, lo, 0.0)))


def _even_in_proj(h, w_in, q_g, k_g, split_maps):
    m = h.shape[0]
    wdt = w_in.shape[1] // 6
    heads = wdt // HEAD_DIM
    tm = min(ATT_BLOCK, m)
    s_mat = _half_mean_matrix()
    row = lambda n, i: (i, 0)
    fix = lambda n, i: (0, 0)
    f32_out = ((m, wdt), F32, (tm, wdt), row)
    bf_out = ((m, wdt), BF16, (tm, wdt), row)
    qg = jnp.tile(q_g, heads).reshape(1, wdt)
    kg = jnp.tile(k_g, heads).reshape(1, wdt)
    norm_extras = lambda g: [(g, (1, wdt), fix), (s_mat, (HEAD_DIM, HEAD_DIM), fix)]

    def ep_plain(acc, e, o):
        o[0][...] = acc

    def ep_silu(acc, e, o):
        o[0][...] = _silu(acc)

    def ep_q(acc, e, o):
        y = _qk_norm(acc, e[0][...], e[1][...]) * (LOG2E * QK_DIM ** -0.5)
        if split_maps:
            y1, y2 = _split_maps(y, _q_aug)
            o[0][...] = y1.astype(BF16)
            o[1][...] = y2.astype(BF16)
        else:
            o[0][...] = y.astype(BF16)

    def ep_k(acc, e, o):
        y = _qk_norm(acc, e[0][...], e[1][...])
        o[0][...] = y
        if split_maps:
            y1, y2 = _split_maps(y, functools.partial(_k_aug, heads=heads))
            o[1][...] = y1.astype(BF16)
            o[2][...] = y2.astype(BF16)

    def ep_v(acc, e, o):
        o[0][...] = acc
        if split_maps:
            o[1][...] = acc.astype(BF16)

    maps = [bf_out, bf_out] if split_maps else []
    (u,) = _linear("in_even_u", [h], w_in, 0, wdt, 1, ep_plain, [], [f32_out], tm)
    (zs,) = _linear("in_even_zs", [h], w_in, 1, wdt, 1, ep_silu, [], [f32_out], tm)
    q = _linear("in_even_q", [h], w_in, 2, wdt, 1, ep_q, norm_extras(qg), maps or [bf_out], tm)
    k = _linear("in_even_k", [h], w_in, 3, wdt, 1, ep_k, norm_extras(kg), [f32_out] + maps, tm)
    v = _linear("in_even_v", [h], w_in, 4, wdt, 1, ep_v, [], [f32_out] + maps[:1], tm)
    (za,) = _linear("in_even_za", [h], w_in, 5, wdt, 1, ep_silu, [], [f32_out], tm)
    return u, zs, q, k, v, za


def _glu_proj(gy, w_glu, zs_gate):
    m, wdt = gy.shape
    tm = min(512, m)

    def ep(acc, e, o):
        o[0][...] = (acc[:, :wdt] * jax.nn.sigmoid(acc[:, wdt:]) * e[0][...]).astype(BF16)

    (out,) = _linear("s5_glu", [gy], w_glu, 0, 2 * wdt, 1, ep,
                     [(zs_gate, (tm, wdt), lambda n, i: (i, 0))],
                     [((m, wdt), BF16, (tm, wdt), lambda n, i: (i, 0))])
    return out


def _out_proj(name, xs, w, x_res, mod):
    m, d = x_res.shape
    tm = min(512, m)
    tn = min(1024, d)
    ncb = d // tn
    gate_col = 2 * ncb
    if mod.shape[0] == 1:
        gate = (mod, (1, tn), lambda n, i: (0, gate_col + n))
    else:
        gate = (mod, (tm, tn), lambda n, i: (i, gate_col + n))

    def ep(acc, e, o):
        o[0][...] = e[0][...] + e[1][...] * acc

    (y,) = _linear(name, xs, w, 0, tn, ncb, ep,
                   [(x_res, (tm, tn), lambda n, i: (i, n)), gate],
                   [((m, d), F32, (tm, tn), lambda n, i: (i, n))])
    return y


def _scan_tables(a_r, a_i, coef_s):
    n = a_r.shape[-1]
    row = lax.broadcasted_iota(jnp.int32, (SUBLANES, n), 0)
    a1 = (jnp.broadcast_to(a_r, (SUBLANES, n)), jnp.broadcast_to(a_i, (SUBLANES, n)))
    cmul = lambda x, y: (x[0] * y[0] - x[1] * y[1], x[0] * y[1] + x[1] * y[0])
    a2 = cmul(a1, a1)
    a4 = cmul(a2, a2)
    pc = a1
    for k in range(1, SUBLANES):
        nxt = cmul(pc, a1)
        pc = (jnp.where(row >= k, nxt[0], pc[0]), jnp.where(row >= k, nxt[1], pc[1]))
    for idx, (p, dist) in enumerate(((a1, 1), (a2, 2), (a4, 4))):
        coef_s[2 * idx] = jnp.where(row >= dist, p[0], 0.0)
        coef_s[2 * idx + 1] = jnp.where(row >= dist, p[1], 0.0)
    coef_s[6] = pc[0]
    coef_s[7] = pc[1]


def _scan_tile(a, bb, carry, coefs):
    m1r, m1i, m2r, m2i, m4r, m4i, pcr, pci = coefs
    cr, ci = carry
    for dist, mr, mi in ((1, m1r, m1i), (2, m2r, m2i), (4, m4r, m4i)):
        sa = pltpu.roll(a, dist, 0)
        sb = pltpu.roll(bb, dist, 0)
        a, bb = a + (mr * sa - mi * sb), bb + (mr * sb + mi * sa)
    return a + (pcr * cr - pci * ci), bb + (pcr * ci + pci * cr)


def _s5_block_kernel(u_ref, h0_ref, m_ref, w_ref, v_ref, d_ref, a8r_ref, a8i_ref, gy_ref, hs_ref,
                     x_s, y_s, st_s, coef_s):
    t = pl.program_id(1)
    rows = x_s.shape[0]
    ns = S5_TILE_STATES

    @pl.when(t == 0)
    def _():
        _scan_tables(a8r_ref[...], a8i_ref[...], coef_s)
        st_s[...] = h0_ref[0]

    u8 = jnp.concatenate([u_ref[pl.ds(s, rows, stride=SUBLANES), :] for s in range(SUBLANES)], axis=1)
    ub = u8.astype(BF16)
    x_s[...] = jnp.dot(ub, w_ref[0], preferred_element_type=F32)

    coefs = [coef_s[k] for k in range(8)]
    row = lax.broadcasted_iota(jnp.int32, (SUBLANES, ns), 0)

    def tile(ti, carry):
        r0 = pl.multiple_of(ti * SUBLANES, SUBLANES)
        a, bb = _scan_tile(x_s[pl.ds(r0, SUBLANES), :ns], x_s[pl.ds(r0, SUBLANES), ns:], carry, coefs)
        x_s[pl.ds(r0, SUBLANES), :ns] = jnp.where(row == 0, carry[0], pltpu.roll(a, 1, 0))
        x_s[pl.ds(r0, SUBLANES), ns:] = jnp.where(row == 0, carry[1], pltpu.roll(bb, 1, 0))
        return a[SUBLANES - 1:SUBLANES, :], bb[SUBLANES - 1:SUBLANES, :]

    cr, ci = lax.fori_loop(0, rows // SUBLANES, tile, (st_s[:, :ns], st_s[:, ns:]))
    st_s[:, :ns] = cr
    st_s[:, ns:] = ci

    y8 = (jnp.dot(ub, m_ref[0], preferred_element_type=F32)
          + jnp.dot(x_s[...].astype(BF16), v_ref[0], preferred_element_type=F32))
    for s in range(SUBLANES):
        ls_ = slice(s * LANES, (s + 1) * LANES)
        y_s[pl.ds(s, rows, stride=SUBLANES), :] = y8[:, ls_] + d_ref[...] * u8[:, ls_]
    gy_ref[...] = jax.nn.gelu(y_s[...]).astype(BF16)

    @pl.when(t == pl.num_programs(1) - 1)
    def _():
        hs_ref[0] = st_s[...]


def _s5_blocked(u, prm):
    lb, wdt = u.shape
    nc = wdt // LANES
    ns = S5_TILE_STATES
    rows = math.gcd(256, lb // SUBLANES)
    nt = lb // (rows * SUBLANES)
    tok = pl.BlockSpec((rows * SUBLANES, LANES), lambda c, t: (t, c))
    op = lambda a: pl.BlockSpec((1,) + a.shape[1:], lambda c, t: (c, 0, 0))
    vec = lambda width: pl.BlockSpec((1, width), lambda c, t: (0, c))
    st = pl.BlockSpec((1, 1, 2 * ns), lambda c, t: (c, 0, 0))
    gy, hs = pl.pallas_call(
        _s5_block_kernel,
        grid=(nc, nt),
        in_specs=[tok, st, op(prm["m"]), op(prm["w"]), op(prm["v"]), vec(LANES), vec(ns), vec(ns)],
        out_specs=[tok, st],
        out_shape=[jax.ShapeDtypeStruct((lb, wdt), BF16), jax.ShapeDtypeStruct((nc, 1, 2 * ns), F32)],
        scratch_shapes=[pltpu.VMEM((rows, 2 * ns), F32), pltpu.VMEM((rows * SUBLANES, LANES), F32),
                        pltpu.VMEM((1, 2 * ns), F32), pltpu.VMEM((8, SUBLANES, ns), F32)],
        compiler_params=_params("arbitrary", "arbitrary"),
        name="s5_blocked",
    )(u, jnp.zeros((nc, 1, 2 * ns), F32), prm["m"], prm["w"], prm["v"], prm["d"], prm["a8r"], prm["a8i"])
    hs = hs.reshape(nc, 2, ns)
    return gy, hs[:, 0].reshape(1, 1, nc * ns), hs[:, 1].reshape(1, 1, nc * ns)


def _s5_kernel(u_ref, h0r_ref, h0i_ref, wbr_ref, wbi_ref, wcr_ref, wci_ref, d_ref, ar_ref, ai_ref,
               gy_ref, hr_ref, hi_ref, xr_s, xi_s, sr_s, si_s, coef_s, *, lane_chunk):
    b, t = pl.program_id(0), pl.program_id(1)
    rows, ns = xr_s.shape
    nc = wbr_ref.shape[0]

    @pl.when((b == 0) & (t == 0))
    def _():
        _scan_tables(ar_ref[...], ai_ref[...], coef_s)

    @pl.when(t == 0)
    def _():
        sr_s[...] = h0r_ref[0]
        si_s[...] = h0i_ref[0]

    ub = u_ref[...].astype(BF16)
    for c in range(nc):
        uc = ub[:, c * LANES:(c + 1) * LANES]
        cs = slice(c * S5_TILE_STATES, (c + 1) * S5_TILE_STATES)
        xr_s[:, cs] = jnp.dot(uc, wbr_ref[c], preferred_element_type=F32)
        xi_s[:, cs] = jnp.dot(uc, wbi_ref[c], preferred_element_type=F32)

    for lc in range(ns // lane_chunk):
        ls = slice(lc * lane_chunk, (lc + 1) * lane_chunk)
        m1r, m1i, m2r, m2i, m4r, m4i, pcr, pci = [coef_s[k, :, ls] for k in range(8)]

        def tile(ti, carry, ls=ls, m1r=m1r, m1i=m1i, m2r=m2r, m2i=m2i, m4r=m4r, m4i=m4i,
                 pcr=pcr, pci=pci):
            cr, ci = carry
            r0 = pl.multiple_of(ti * SUBLANES, SUBLANES)
            a = xr_s[pl.ds(r0, SUBLANES), ls]
            bb = xi_s[pl.ds(r0, SUBLANES), ls]
            for dist, mr, mi in ((1, m1r, m1i), (2, m2r, m2i), (4, m4r, m4i)):
                sa = pltpu.roll(a, dist, 0)
                sb = pltpu.roll(bb, dist, 0)
                a, bb = a + (mr * sa - mi * sb), bb + (mr * sb + mi * sa)
            a, bb = a + (pcr * cr - pci * ci), bb + (pcr * ci + pci * cr)
            xr_s[pl.ds(r0, SUBLANES), ls] = a
            xi_s[pl.ds(r0, SUBLANES), ls] = bb
            return a[SUBLANES - 1:SUBLANES, :], bb[SUBLANES - 1:SUBLANES, :]

        cr, ci = lax.fori_loop(0, rows // SUBLANES, tile, (sr_s[:, ls], si_s[:, ls]))
        sr_s[:, ls] = cr
        si_s[:, ls] = ci

    for c in range(nc):
        cs = slice(c * S5_TILE_STATES, (c + 1) * S5_TILE_STATES)
        os_ = slice(c * LANES, (c + 1) * LANES)
        y = (jnp.dot(xr_s[:, cs].astype(BF16), wcr_ref[c], preferred_element_type=F32)
             - jnp.dot(xi_s[:, cs].astype(BF16), wci_ref[c], preferred_element_type=F32)
             + d_ref[:, os_] * u_ref[:, os_])
        gy_ref[:, os_] = jax.nn.gelu(y).astype(BF16)

    @pl.when(t == pl.num_programs(1) - 1)
    def _():
        hr_ref[0] = sr_s[...]
        hi_ref[0] = si_s[...]


def _s5(u, h0r, h0i, prm, nb, lb):
    wdt = u.shape[1]
    ns = h0r.shape[-1]
    nc = wdt // LANES
    rows = min(256, lb)
    nt = lb // rows
    full = lambda a: pl.BlockSpec(a.shape, lambda b, t, nd=a.ndim: (0,) * nd)
    st = pl.BlockSpec((1, 1, ns), lambda b, t: (b, 0, 0))
    tok = pl.BlockSpec((rows, wdt), lambda b, t: (b * nt + t, 0))
    return pl.pallas_call(
        functools.partial(_s5_kernel, lane_chunk=min(512, ns)),
        grid=(nb, nt),
        in_specs=[tok, st, st] + [full(prm[k]) for k in ("wbr", "wbi", "wcr", "wci", "d", "ar", "ai")],
        out_specs=[tok, st, st],
        out_shape=[jax.ShapeDtypeStruct((nb * lb, wdt), BF16),
                   jax.ShapeDtypeStruct((nb, 1, ns), F32),
                   jax.ShapeDtypeStruct((nb, 1, ns), F32)],
        scratch_shapes=[pltpu.VMEM((rows, ns), F32), pltpu.VMEM((rows, ns), F32),
                        pltpu.VMEM((1, ns), F32), pltpu.VMEM((1, ns), F32),
                        pltpu.VMEM((8, SUBLANES, ns), F32)],
        compiler_params=_params("arbitrary", "arbitrary"),
        name="s5_scan",
    )(u, h0r, h0i, prm["wbr"], prm["wbi"], prm["wcr"], prm["wci"], prm["d"], prm["ar"], prm["ai"])


def _s5_params(lam_re, lam_im, log_dt, b_re, b_im, c_re, c_im, d_skip):
    g, p = lam_re.shape
    nc = g // S5_LANE_GROUPS
    dt = jnp.exp(log_dt)[:, None]
    mag = jnp.exp(lam_re * dt)
    ar = mag * jnp.cos(lam_im * dt)
    ai = mag * jnp.sin(lam_im * dt)
    den = lam_re * lam_re + lam_im * lam_im
    zr = ((ar - 1.0) * lam_re + ai * lam_im) / den
    zi = (ai * lam_re - (ar - 1.0) * lam_im) / den
    bbr = zr[..., None] * b_re - zi[..., None] * b_im
    bbi = zr[..., None] * b_im + zi[..., None] * b_re
    eye = jnp.eye(S5_LANE_GROUPS, dtype=F32)

    def bdiag_in(x):
        t = x.reshape(nc, S5_LANE_GROUPS, p, S5_GROUP).transpose(0, 1, 3, 2)
        return (t[:, :, :, None, :] * eye[None, :, None, :, None]).reshape(nc, LANES, S5_TILE_STATES).astype(BF16)

    def bdiag_out(x):
        t = x.reshape(nc, S5_LANE_GROUPS, S5_GROUP, p).transpose(0, 1, 3, 2)
        return (t[:, :, :, None, :] * eye[None, :, None, :, None]).reshape(nc, S5_TILE_STATES, LANES).astype(BF16)

    prm = dict(wbr=bdiag_in(bbr), wbi=bdiag_in(bbi), wcr=bdiag_out(c_re), wci=bdiag_out(c_im),
               d=d_skip.reshape(1, g * S5_GROUP), ar=ar.reshape(1, g * p), ai=ai.reshape(1, g * p))
    prm.update(_s5_block_operators(ar, ai, bbr, bbi, c_re, c_im))
    return prm


def _s5_block_operators(ar, ai, bbr, bbi, c_re, c_im):
    g, p = ar.shape
    nc = g // S5_LANE_GROUPS
    t8 = SUBLANES
    hp = lax.Precision.HIGHEST
    pr, pi = [jnp.ones_like(ar)], [jnp.zeros_like(ai)]
    for _ in range(t8):
        pr, pi = pr + [pr[-1] * ar - pi[-1] * ai], pi + [pr[-1] * ai + pi[-1] * ar]
    pw_r, pw_i = jnp.stack(pr), jnp.stack(pi)

    def c_times(lo):
        a_r, a_i = pw_r[lo:lo + t8, :, None, :], pw_i[lo:lo + t8, :, None, :]
        return c_re[None] * a_r - c_im[None] * a_i, c_re[None] * a_i + c_im[None] * a_r

    def block_diagonal(compact, row_group, lane_group):
        rows = compact.shape[1]
        width = S5_LANE_GROUPS * LANES
        lane = jnp.arange(width)
        src = jnp.arange(LANES)
        same = ((lane // (S5_LANE_GROUPS * lane_group))[None, :] == (src // lane_group)[:, None]) \
            & ((lane % lane_group)[None, :] == (src % lane_group)[:, None])
        spread = jnp.einsum("crl,lw->crw", compact.astype(BF16), same.astype(BF16),
                            preferred_element_type=F32)
        row_i = (jnp.arange(rows) // row_group) % S5_LANE_GROUPS
        lane_j = (lane // lane_group) % S5_LANE_GROUPS
        return jnp.where(row_i[:, None] == lane_j[None, :], spread, 0.0).astype(BF16)

    e_r, e_i = c_times(0)
    k = (jnp.einsum("jgop,gpi->jgoi", e_r, bbr, precision=hp)
         - jnp.einsum("jgop,gpi->jgoi", e_i, bbi, precision=hp))
    zero = jnp.zeros_like(k[0])
    toe = jnp.stack([jnp.stack([k[t - s] if t >= s else zero for t in range(t8)])
                     for s in range(t8)])
    toe = toe.reshape(t8, t8, nc, S5_LANE_GROUPS, S5_GROUP, S5_GROUP)
    m = block_diagonal(toe.transpose(2, 0, 3, 5, 1, 4).reshape(nc, t8 * LANES, LANES),
                       S5_GROUP, S5_GROUP)

    rev_r, rev_i = jnp.stack(pr[t8 - 1::-1]), jnp.stack(pi[t8 - 1::-1])
    ab_r = rev_r[..., None] * bbr[None] - rev_i[..., None] * bbi[None]
    ab_i = rev_r[..., None] * bbi[None] + rev_i[..., None] * bbr[None]
    ab = jnp.stack([ab_r, ab_i]).reshape(2, t8, nc, S5_LANE_GROUPS, p, S5_GROUP)
    w = block_diagonal(ab.transpose(2, 1, 3, 5, 0, 4).reshape(nc, t8 * LANES, 2 * p), S5_GROUP, p)

    f_r, f_i = c_times(1)
    cf = jnp.stack([f_r, -f_i]).reshape(2, t8, nc, S5_LANE_GROUPS, S5_GROUP, p)
    v = block_diagonal(cf.transpose(2, 0, 3, 5, 1, 4).reshape(nc, 2 * S5_TILE_STATES, LANES),
                       p, S5_GROUP)

    return dict(m=m, w=w, v=v, a8r=pw_r[t8].reshape(1, g * p), a8i=pw_i[t8].reshape(1, g * p))


def _stack_maps(q):
    lane = lax.broadcasted_iota(jnp.int32, q.shape, 1)
    zero = jnp.zeros_like(q)
    return jnp.concatenate([jnp.where(lane < QK_DIM, q, zero), jnp.where(lane >= QK_DIM, q, zero)], axis=0)


def _qk(qq, kb):
    return lax.dot_general(qq, kb, (((1,), (1,)), ((), ())), preferred_element_type=F32)


def _attn_finish(o, g_row, gate, lam_init):
    ms = jnp.mean(o * o, axis=-1, keepdims=True)
    return (o * lax.rsqrt(ms + EPS) * g_row * (1.0 - lam_init) * gate).astype(BF16)


def _attn_kernel(slopes_ref, lam_ref, q1_ref, q2_ref, k1_ref, k2_ref, vt_ref, za_ref, og_ref, o_ref,
                 m_s, l_s, acc_s, sa_s, sb_s, ma_s, mb_s, dbias_s, *, lam_init):
    h, i = pl.program_id(0), pl.program_id(1)
    t = q1_ref.shape[0]
    nq = 2 * t
    slope = slopes_ref[h]

    @pl.when(i == 0)
    def _():
        k_local = lax.broadcasted_iota(jnp.int32, (t, nq), 0)
        q_col = lax.broadcasted_iota(jnp.int32, (t, nq), 1)
        q_local = jnp.where(q_col >= t, q_col - t, q_col)
        vis = (k_local // CHUNK) <= (q_local // CHUNK)
        dist = jnp.abs(q_local - k_local).astype(F32)
        dbias_s[...] = (jnp.where(vis, -slope * dist, NEG_INF)
                        + slope * (q_local - k_local).astype(F32))

    q1 = q1_ref[...]
    q2 = q2_ref[...]
    m_s[...] = jnp.full(m_s.shape, -jnp.inf, F32)
    l_s[...] = jnp.zeros(l_s.shape, F32)
    acc_s[...] = jnp.zeros(acc_s.shape, F32)

    def scores(blk, diagonal, s_ref, mx_ref):
        r0 = pl.multiple_of(blk * t, t)
        for half, (k_ref, q) in enumerate(((k1_ref, q1), (k2_ref, q2))):
            cols = slice(half * t, (half + 1) * t)
            s_t = _qk(k_ref[pl.ds(r0, t), :], q)
            if diagonal:
                s_t = s_t + dbias_s[:, cols]
            s_ref[:, cols] = s_t
            mx_ref[:, cols] = jnp.max(s_t, axis=0, keepdims=True)

    def consume(blk, s_ref, mx_ref):
        r0 = pl.multiple_of(blk * t, t)
        shift = slope * ((blk - i) * t).astype(F32)
        m_old = m_s[...]
        m_new = jnp.maximum(m_old, mx_ref[...] + shift)
        alpha = jnp.exp2(m_old - m_new)
        p = jnp.exp2(s_ref[...] - (m_new - shift))
        l_s[...] = alpha * l_s[...] + jnp.sum(p, axis=0, keepdims=True)
        acc_s[...] = alpha * acc_s[...] + jnp.dot(vt_ref[:, pl.ds(r0, t)], p.astype(BF16),
                                                  preferred_element_type=F32)
        m_s[...] = m_new

    scores(i, True, sa_s, ma_s)
    pairs = i // 2

    def pair(jj, carry):
        j0 = 2 * jj
        scores(j0, False, sb_s, mb_s)
        consume(jnp.where(jj == 0, i, j0 - 1), sa_s, ma_s)
        scores(j0 + 1, False, sa_s, ma_s)
        consume(j0, sb_s, mb_s)
        return carry

    lax.fori_loop(0, pairs, pair, 0)
    in_a = jnp.where(pairs == 0, i, 2 * pairs - 1)

    @pl.when(i % 2 == 1)
    def _():
        scores(i - 1, False, sb_s, mb_s)
        consume(in_a, sa_s, ma_s)
        consume(i - 1, sb_s, mb_s)

    @pl.when(i % 2 == 0)
    def _():
        consume(in_a, sa_s, ma_s)

    acc = acc_s[...]
    l = l_s[...]
    o_t = acc[:, :t] / l[:, :t] - lam_ref[0] * (acc[:, t:] / l[:, t:])
    ms = jnp.mean(o_t * o_t, axis=0, keepdims=True)
    y = (o_t * lax.rsqrt(ms + EPS)).T
    o_ref[...] = (y * og_ref[...] * (1.0 - lam_init) * za_ref[...]).astype(BF16)


def _attention_prompt(q1, q2, k1, k2, vb, za_gate, out_g, lam, lam_init):
    m, wdt = q1.shape
    heads = wdt // HEAD_DIM
    t = min(ATT_BLOCK, m)
    slopes = LOG2E * 2.0 ** (-8.0 * jnp.arange(1, heads + 1, dtype=F32) / heads)
    smem = pl.BlockSpec(memory_space=pltpu.SMEM)
    blk = pl.BlockSpec((t, HEAD_DIM), lambda h, i: (i, h))
    keys = pl.BlockSpec((m, HEAD_DIM), lambda h, i: (0, h))
    return pl.pallas_call(
        functools.partial(_attn_kernel, lam_init=lam_init),
        grid=(heads, m // t),
        in_specs=[smem, smem, blk, blk, keys, keys,
                  pl.BlockSpec((HEAD_DIM, m), lambda h, i: (h, 0)),
                  blk, pl.BlockSpec((1, HEAD_DIM), lambda h, i: (0, 0))],
        out_specs=blk,
        out_shape=jax.ShapeDtypeStruct((m, wdt), BF16),
        scratch_shapes=[pltpu.VMEM((1, 2 * t), F32), pltpu.VMEM((1, 2 * t), F32),
                        pltpu.VMEM((HEAD_DIM, 2 * t), F32),
                        pltpu.VMEM((t, 2 * t), F32), pltpu.VMEM((t, 2 * t), F32),
                        pltpu.VMEM((1, 2 * t), F32), pltpu.VMEM((1, 2 * t), F32),
                        pltpu.VMEM((t, 2 * t), F32)],
        compiler_params=_params("arbitrary", "arbitrary"),
        name="attn_prompt",
    )(slopes, lam, q1, q2, k1, k2, vb.T, za_gate, out_g.reshape(1, HEAD_DIM))


MASKED_DIST = 1e32


def _attn_cached_kernel(lam_ref, q_ref, kn_ref, vn_ref, kc_ref, vc_ref, za_ref, og_ref, o_ref,
                        dc_s, dn_s, *, lam_init, heads):
    ls = q_ref.shape[0]
    past = kc_ref.shape[2]
    lam = lam_ref[0]

    @pl.when(pl.program_id(0) == 0)
    def _():
        def table(n, k0):
            q_row = lax.broadcasted_iota(jnp.int32, (2 * ls, n), 0)
            q_pos = past + jnp.where(q_row >= ls, q_row - ls, q_row)
            k_pos = k0 + lax.broadcasted_iota(jnp.int32, (2 * ls, n), 1)
            vis = (k_pos // CHUNK) <= (q_pos // CHUNK)
            return jnp.where(vis, jnp.abs(q_pos - k_pos).astype(F32), MASKED_DIST)

        dc_s[...] = table(past, 0)
        dn_s[...] = table(ls, past)

    for h in range(heads):
        hs = slice(h * HEAD_DIM, (h + 1) * HEAD_DIM)
        slope = LOG2E * 2.0 ** (-8.0 * (h + 1) / heads)
        qq = _stack_maps(q_ref[:, hs])
        sc = _qk(qq, kc_ref[0, h]) - slope * dc_s[...]
        sn = _qk(qq, kn_ref[:, hs].astype(BF16)) - slope * dn_s[...]
        mx = jnp.maximum(jnp.max(sc, axis=1, keepdims=True), jnp.max(sn, axis=1, keepdims=True))
        pc = jnp.exp2(sc - mx)
        pn = jnp.exp2(sn - mx)
        inv = 1.0 / (jnp.sum(pc, axis=1, keepdims=True) + jnp.sum(pn, axis=1, keepdims=True))
        pc = pc * inv
        pn = pn * inv
        wc = (pc[:ls] - lam * pc[ls:]).astype(BF16)
        wn = (pn[:ls] - lam * pn[ls:]).astype(BF16)
        o = (jnp.dot(wc, vc_ref[0, h], preferred_element_type=F32)
             + jnp.dot(wn, vn_ref[:, hs].astype(BF16), preferred_element_type=F32))
        o_ref[:, hs] = _attn_finish(o, og_ref[...], za_ref[:, hs], lam_init)


def _attention_cached(q, k_new, v_new, cache_k, cache_v, za_gate, out_g, lam, lam_init, nb, ls):
    wdt = q.shape[1]
    heads = wdt // HEAD_DIM
    past = cache_k.shape[1]
    head_major = lambda a: jnp.transpose(a.astype(BF16), (0, 2, 1, 3))
    tok = pl.BlockSpec((ls, wdt), lambda b: (b, 0))
    cache = pl.BlockSpec((1, heads, past, HEAD_DIM), lambda b: (b, 0, 0, 0))
    return pl.pallas_call(
        functools.partial(_attn_cached_kernel, lam_init=lam_init, heads=heads),
        grid=(nb,),
        in_specs=[pl.BlockSpec(memory_space=pltpu.SMEM), tok, tok, tok, cache, cache, tok,
                  pl.BlockSpec((1, HEAD_DIM), lambda b: (0, 0))],
        out_specs=tok,
        out_shape=jax.ShapeDtypeStruct((nb * ls, wdt), BF16),
        scratch_shapes=[pltpu.VMEM((2 * ls, past), F32), pltpu.VMEM((2 * ls, ls), F32)],
        compiler_params=_params("arbitrary"),
        name="attn_cached",
    )(lam, q, k_new, v_new, head_major(cache_k), head_major(cache_v), za_gate,
      out_g.reshape(1, HEAD_DIM))


def _odd_in_proj(h, w_in):
    m, d = h.shape
    tm = min(512, m)
    tn = min(512, d)
    ncb = d // tn

    def kernel(h_ref, wa_ref, wb_ref, wz_ref, g_ref, z_ref):
        x = h_ref[...]
        a = jnp.dot(x, wa_ref[...], preferred_element_type=F32)
        b = jnp.dot(x, wb_ref[...], preferred_element_type=F32)
        z = jnp.dot(x, wz_ref[...], preferred_element_type=F32)
        g_ref[...] = a * jax.nn.sigmoid(b)
        z_ref[...] = _silu(z)

    wspec = lambda off: pl.BlockSpec((d, tn), lambda n, i, off=off: (0, n + off))
    out = pl.BlockSpec((tm, tn), lambda n, i: (i, n))
    return pl.pallas_call(
        kernel,
        grid=(ncb, m // tm),
        in_specs=[pl.BlockSpec((tm, d), lambda n, i: (i, 0)), wspec(0), wspec(ncb), wspec(2 * ncb)],
        out_specs=[out, out],
        out_shape=[jax.ShapeDtypeStruct((m, d), F32), jax.ShapeDtypeStruct((m, d), F32)],
        compiler_params=_params("arbitrary", "arbitrary"),
        name="in_odd",
    )(h, w_in, w_in, w_in)


CONV_HALO = 32


LN_ROWS = 64


def _conv_kernel(cur_ref, prev_ref, ctx_ref, zg_ref, w_ref, b_ref, lg_ref, lb_ref, o_ref, xp_s, y_s,
                 sh_s, *, row_chunk):
    t = pl.program_id(1)
    tm, c = cur_ref.shape

    @pl.when(t == 0)
    def _():
        xp_s[0:CONV_HALO, :] = ctx_ref[0]

    @pl.when(t > 0)
    def _():
        xp_s[0:CONV_HALO, :] = prev_ref[...]

    xp_s[CONV_HALO:CONV_HALO + tm, :] = cur_ref[...]
    xp_s[CONV_HALO + tm:, :] = jnp.zeros((SUBLANES, c), F32)

    first = CONV_HALO - (CONV_W - 1)
    span = row_chunk + CONV_HALO

    def lane_body(lc, carry):
        c0 = pl.multiple_of(lc * LANES, LANES)

        def row_body(rc, carry2):
            r0 = pl.multiple_of(rc * row_chunk, row_chunk)
            blk = xp_s[pl.ds(r0, span + SUBLANES), pl.ds(c0, LANES)]
            for phase in range(SUBLANES):
                sh_s[phase] = blk[phase:phase + span]
            acc = jnp.zeros((row_chunk, LANES), F32)
            for phase in range(SUBLANES):
                for a in range(span // SUBLANES):
                    tap = SUBLANES * a + phase - first
                    if 0 <= tap < CONV_W:
                        acc = acc + (w_ref[pl.ds(tap, 1), pl.ds(c0, LANES)]
                                     * sh_s[phase, SUBLANES * a:SUBLANES * a + row_chunk, :])
            y_s[pl.ds(r0, row_chunk), pl.ds(c0, LANES)] = acc
            return carry2

        return lax.fori_loop(0, tm // row_chunk, row_body, carry)

    lax.fori_loop(0, c // LANES, lane_body, 0)

    ln_rows = min(LN_ROWS, tm)

    def ln_body(r, carry):
        rows = pl.ds(pl.multiple_of(r * ln_rows, ln_rows), ln_rows)
        y = y_s[rows, :] + b_ref[...]
        mu = jnp.mean(y, axis=-1, keepdims=True)
        yc = y - mu
        var = jnp.mean(yc * yc, axis=-1, keepdims=True)
        yn = yc * lax.rsqrt(var + EPS) * lg_ref[...] + lb_ref[...]
        o_ref[rows, :] = (_silu(yn) * zg_ref[rows, :]).astype(BF16)
        return carry

    lax.fori_loop(0, tm // ln_rows, ln_body, 0)


def _conv_module(g, ctx, zg, conv_w, conv_b, ln_g, ln_b, nb, lb):
    c = g.shape[1]
    tm = min(256, lb)
    nt = lb // tm
    halo_blocks = tm // CONV_HALO
    w_pad = jnp.zeros((CONV_HALO, c), F32).at[:CONV_W].set(conv_w)
    cur = pl.BlockSpec((tm, c), lambda b, t: (b * nt + t, 0))
    prev = pl.BlockSpec((CONV_HALO, c), lambda b, t: (jnp.maximum((b * nt + t) * halo_blocks - 1, 0), 0))
    vec = pl.BlockSpec((1, c), lambda b, t: (0, 0))
    return pl.pallas_call(
        functools.partial(_conv_kernel, row_chunk=min(64, tm)),
        grid=(nb, nt),
        in_specs=[cur, prev, pl.BlockSpec((1, CONV_HALO, c), lambda b, t: (b, 0, 0)), cur,
                  pl.BlockSpec((CONV_HALO, c), lambda b, t: (0, 0)), vec, vec, vec],
        out_specs=cur,
        out_shape=jax.ShapeDtypeStruct((nb * lb, c), BF16),
        scratch_shapes=[pltpu.VMEM((CONV_HALO + tm + SUBLANES, c), F32), pltpu.VMEM((tm, c), F32),
                        pltpu.VMEM((SUBLANES, min(64, tm) + CONV_HALO, LANES), F32)],
        compiler_params=_params("arbitrary", "arbitrary"),
        name="conv_module",
    )(g, g, ctx, zg, w_pad, conv_b.reshape(1, c), ln_g.reshape(1, c), ln_b.reshape(1, c))


def kernel(x_prompt, x_sample, c_prompt, c_sample, cache_k, cache_v, state_s5_re, state_s5_im, state_conv, norm_g, w_ada, b_ada, w_in_even, w_out_even, s5_lam_re, s5_lam_im, s5_log_dt, s5_b_re, s5_b_im, s5_c_re, s5_c_im, s5_d, s5_w_glu, q_norm_g, k_norm_g, lam_q1, lam_k1, lam_q2, lam_k2, attn_out_g, w_in_odd, conv_w, conv_b, conv_ln_g, conv_ln_b, w_out_odd):
    bp, lp, d = x_prompt.shape
    bs, ls, _ = x_sample.shape
    assert bp == 1 and w_ada.shape[0] == 2, "one prompt sequence, one even and one odd layer"
    wdt = d // 2
    heads = wdt // HEAD_DIM
    ns = (wdt // S5_GROUP) * S5_STATE
    ms = bs * ls

    xp = x_prompt.reshape(lp, d)
    xs = x_sample.reshape(ms, d)

    rows = bp + bs
    rows_pad = -(-rows // 16) * 16
    c_all = jnp.concatenate([c_prompt, c_sample, jnp.zeros((rows_pad - rows, d), F32)], axis=0)
    mod = _ada_mod(c_all, w_ada, b_ada)
    mod_p = [mod[l, :bp] for l in range(2)]
    mod_s = [jnp.repeat(mod[l, bp:rows], ls, axis=0) for l in range(2)]

    w_in_e = w_in_even[0].astype(BF16)
    w_out_e = w_out_even[0].astype(BF16)
    w_glu = s5_w_glu[0].astype(BF16)
    w_in_o = w_in_odd[0].astype(BF16)
    w_out_o = w_out_odd[0].astype(BF16)

    s5p = _s5_params(s5_lam_re[0], s5_lam_im[0], s5_log_dt[0], s5_b_re[0], s5_b_im[0],
                     s5_c_re[0], s5_c_im[0], s5_d[0])
    lam_init = 0.8 - 0.6 * math.exp(-0.3 * 0)
    lam = (jnp.exp(jnp.sum(lam_q1[0] * lam_k1[0])) - jnp.exp(jnp.sum(lam_q2[0] * lam_k2[0]))
           + lam_init).reshape(1).astype(F32)

    def even_layer(x, modl, nb, lb, h0r, h0i, cache):
        h = _modnorm(x, modl, norm_g[0])
        u, zs, q, k, v, za = _even_in_proj(h, w_in_e, q_norm_g[0], k_norm_g[0], cache is None)
        gy, hr, hi = _s5_blocked(u, s5p) if cache is None else _s5(u, h0r, h0i, s5p, nb, lb)
        s5_out = _glu_proj(gy, w_glu, zs)
        if cache is None:
            att = _attention_prompt(q[0], q[1], k[1], k[2], v[1], za, attn_out_g[0], lam, lam_init)
        else:
            att = _attention_cached(q[0], k[0], v[0], cache[0], cache[1], za, attn_out_g[0], lam,
                                    lam_init, nb, lb)
        y = _out_proj("out_even", [s5_out, att], w_out_e, x, modl)
        return y, k[0], v[0], hr, hi

    def odd_layer(x, modl, nb, lb, ctx):
        h = _modnorm(x, modl, norm_g[1])
        g, zg = _odd_in_proj(h, w_in_o)
        yc = _conv_module(g, ctx, zg, conv_w[0], conv_b[0], conv_ln_g[0], conv_ln_b[0], nb, lb)
        y = _out_proj("out_odd", [yc], w_out_o, x, modl)
        return y, g

    zero_state = jnp.zeros((bp, 1, ns), F32)
    yp, kp, vp, hrp, hip = even_layer(xp, mod_p[0], bp, lp, zero_state, zero_state, None)
    past = cache_k.shape[2]
    ys, ksn, vsn, hrs, his = even_layer(
        xs, mod_s[0], bs, ls, state_s5_re[0].reshape(bs, 1, ns), state_s5_im[0].reshape(bs, 1, ns),
        (cache_k[0], cache_v[0]))

    pad = CONV_HALO - (CONV_W - 1)
    ctx_p = jnp.zeros((bp, CONV_HALO, d), F32)
    ctx_s = jnp.pad(state_conv[0], ((0, 0), (pad, 0), (0, 0)))
    yp, gp = odd_layer(yp, mod_p[1], bp, lp, ctx_p)
    ys, gs = odd_layer(ys, mod_s[1], bs, ls, ctx_s)

    groups = wdt // S5_GROUP
    tail = CONV_W - 1
    conv_p = gp.reshape(bp, lp, d)[:, lp - tail:]
    conv_s = jnp.concatenate([state_conv[0], gs.reshape(bs, ls, d)], axis=1)[:, -tail:]
    return (yp.reshape(bp, lp, d), ys.reshape(bs, ls, d),
            kp.reshape(1, bp, lp, heads, HEAD_DIM), vp.reshape(1, bp, lp, heads, HEAD_DIM),
            hrp.reshape(1, bp, groups, S5_STATE), hip.reshape(1, bp, groups, S5_STATE), conv_p[None],
            ksn.reshape(1, bs, ls, heads, HEAD_DIM), vsn.reshape(1, bs, ls, heads, HEAD_DIM),
            hrs.reshape(1, bs, groups, S5_STATE), his.reshape(1, bs, groups, S5_STATE), conv_s[None])
```

```python
import functools
import math

import jax
import jax.numpy as jnp
from jax import lax
from jax.experimental import pallas as pl
from jax.experimental.pallas import tpu as pltpu

F32 = jnp.float32
BF16 = jnp.bfloat16

CHUNK = 64
HEAD_DIM = 128
QK_DIM = 64
S5_GROUP = 16
S5_STATE = 64
CONV_W = 31
EPS = 1e-6
NEG_INF = -1e30
LOG2E = 1.4426950408889634

LANES = 128
SUBLANES = 8
ATT_BLOCK = 512
S5_LANE_GROUPS = LANES // S5_GROUP
S5_TILE_STATES = S5_LANE_GROUPS * S5_STATE
VMEM_LIMIT = 48 * 1024 * 1024


def _params(*sem):
    return pltpu.CompilerParams(dimension_semantics=sem, vmem_limit_bytes=VMEM_LIMIT)


def _silu(x):
    return x * jax.nn.sigmoid(x)


def _ada_kernel(c_ref, w_ref, b_ref, o_ref):
    a = _silu(c_ref[...]).astype(BF16)
    w = w_ref[0].astype(BF16)
    o_ref[0] = jnp.dot(a, w, preferred_element_type=F32) + b_ref[0]


def _ada_mod(c_all, w_ada, b_ada):
    depth, d, n = w_ada.shape
    rows = c_all.shape[0]
    tn = 512
    return pl.pallas_call(
        _ada_kernel,
        grid=(depth, n // tn),
        in_specs=[pl.BlockSpec((rows, d), lambda l, j: (0, 0)),
                  pl.BlockSpec((1, d, tn), lambda l, j: (l, 0, j)),
                  pl.BlockSpec((1, 1, tn), lambda l, j: (l, 0, j))],
        out_specs=pl.BlockSpec((1, rows, tn), lambda l, j: (l, 0, j)),
        out_shape=jax.ShapeDtypeStruct((depth, rows, n), F32),
        compiler_params=_params("arbitrary", "arbitrary"),
        name="ada_mod",
    )(c_all, w_ada, b_ada.reshape(depth, 1, n))


def _modnorm_kernel(x_ref, sh_ref, sc_ref, g_ref, o_ref):
    x = x_ref[...]
    ms = jnp.mean(x * x, axis=-1, keepdims=True)
    y = x * lax.rsqrt(ms + EPS) * g_ref[...]
    o_ref[...] = (y * (1.0 + sc_ref[...]) + sh_ref[...]).astype(BF16)


def _row_spec(arr, tm, width, col_block):
    if arr.shape[0] == 1:
        return pl.BlockSpec((1, width), lambda i, c=col_block: (0, c))
    return pl.BlockSpec((tm, width), lambda i, c=col_block: (i, c))


def _modnorm(x, mod, g):
    m, d = x.shape
    tm = min(512, m)
    return pl.pallas_call(
        _modnorm_kernel,
        grid=(m // tm,),
        in_specs=[pl.BlockSpec((tm, d), lambda i: (i, 0)),
                  _row_spec(mod, tm, d, 0),
                  _row_spec(mod, tm, d, 1),
                  pl.BlockSpec((1, d), lambda i: (0, 0))],
        out_specs=pl.BlockSpec((tm, d), lambda i: (i, 0)),
        out_shape=jax.ShapeDtypeStruct((m, d), BF16),
        compiler_params=_params("arbitrary"),
        name="modnorm",
    )(x, mod, mod, g.reshape(1, d))


def _linear(name, xs, w, col0, ncols, ncb, epilogue, extras, outs, tm=512):
    m = xs[0].shape[0]
    tm = min(tm, m)
    k_total = w.shape[0]
    nx, ne = len(xs), len(extras)

    def kernel(*refs):
        x_refs, w_ref = refs[:nx], refs[nx]
        e_refs, o_refs, wb_s = refs[nx + 1:nx + 1 + ne], refs[nx + 1 + ne:-1], refs[-1]

        @pl.when(pl.program_id(1) == 0)
        def _():
            wb_s[...] = w_ref[...].astype(BF16)

        acc, k0 = None, 0
        for xr in x_refs:
            kp = xr.shape[1]
            part = jnp.dot(xr[...], wb_s[k0:k0 + kp, :], preferred_element_type=F32)
            acc = part if acc is None else acc + part
            k0 += kp
        epilogue(acc, e_refs, o_refs)

    in_specs = [pl.BlockSpec((tm, x.shape[1]), lambda n, i: (i, 0)) for x in xs]
    in_specs.append(pl.BlockSpec((k_total, ncols), lambda n, i: (0, col0 + n)))
    in_specs += [pl.BlockSpec(bs, im) for _, bs, im in extras]
    res = pl.pallas_call(
        kernel,
        grid=(ncb, m // tm),
        in_specs=in_specs,
        out_specs=[pl.BlockSpec(bs, im) for _, _, bs, im in outs],
        out_shape=[jax.ShapeDtypeStruct(s, dt) for s, dt, _, _ in outs],
        scratch_shapes=[pltpu.VMEM((k_total, ncols), BF16)],
        compiler_params=_params("arbitrary", "arbitrary"),
        name=name,
    )(*xs, w, *[a for a, _, _ in extras])
    return res


def _half_mean_matrix():
    r = lax.broadcasted_iota(jnp.int32, (HEAD_DIM, HEAD_DIM), 0) // QK_DIM
    c = lax.broadcasted_iota(jnp.int32, (HEAD_DIM, HEAD_DIM), 1) // QK_DIM
    return jnp.where(r == c, 1.0 / QK_DIM, 0.0).astype(BF16)


def _qk_norm(acc, g_row, s_mat):
    heads = acc.shape[1] // HEAD_DIM
    outs = []
    for h in range(heads):
        xh = acc[:, h * HEAD_DIM:(h + 1) * HEAD_DIM]
        t = xh * xh
        hi = t.astype(BF16)
        lo = (t - hi.astype(F32)).astype(BF16)
        ms = (jnp.dot(hi, s_mat, preferred_element_type=F32)
              + jnp.dot(lo, s_mat, preferred_element_type=F32))
        outs.append(xh * lax.rsqrt(ms + EPS))
    return jnp.concatenate(outs, axis=1) * g_row


ALIBI_LANES = 3


def _split_maps(y, aug_fn):
    tm = y.shape[0]
    lane = lax.broadcasted_iota(jnp.int32, (tm, HEAD_DIM), 1)
    first, second = [], []
    for h in range(y.shape[1] // HEAD_DIM):
        yh = y[:, h * HEAD_DIM:(h + 1) * HEAD_DIM]
        aug = aug_fn(h, lane)
        first.append(jnp.where(lane < QK_DIM, yh, aug))
        second.append(jnp.where(lane < QK_DIM, pltpu.roll(yh, QK_DIM, 1), aug))
    return jnp.concatenate(first, axis=1), jnp.concatenate(second, axis=1)


def _q_aug(h, lane):
    return jnp.where(lane < QK_DIM + ALIBI_LANES, 1.0, 0.0)


def _k_aug(h, lane, heads):
    slope = LOG2E * 2.0 ** (-8.0 * (h + 1) / heads)
    b = slope * lax.broadcasted_iota(jnp.int32, lane.shape, 0).astype(F32)
    hi = b.astype(BF16).astype(F32)
    mid = (b - hi).astype(BF16).astype(F32)
    lo = b - hi - mid
    return jnp.where(lane == QK_DIM, hi, jnp.where(lane == QK_DIM + 1, mid,
                                                   jnp.where(lane == QK_DIM + 2, lo, 0.0)))


def _even_in_proj(h, w_in, q_g, k_g, split_maps):
    m = h.shape[0]
    wdt = w_in.shape[1] // 6
    heads = wdt // HEAD_DIM
    tm = min(ATT_BLOCK, m)
    s_mat = _half_mean_matrix()
    row = lambda n, i: (i, 0)
    fix = lambda n, i: (0, 0)
    f32_out = ((m, wdt), F32, (tm, wdt), row)
    bf_out = ((m, wdt), BF16, (tm, wdt), row)
    qg = jnp.tile(q_g, heads).reshape(1, wdt)
    kg = jnp.tile(k_g, heads).reshape(1, wdt)
    norm_extras = lambda g: [(g, (1, wdt), fix), (s_mat, (HEAD_DIM, HEAD_DIM), fix)]

    def ep_plain(acc, e, o):
        o[0][...] = acc

    def ep_silu(acc, e, o):
        o[0][...] = _silu(acc)

    def ep_q(acc, e, o):
        y = _qk_norm(acc, e[0][...], e[1][...]) * (LOG2E * QK_DIM ** -0.5)
        if split_maps:
            y1, y2 = _split_maps(y, _q_aug)
            o[0][...] = y1.astype(BF16)
            o[1][...] = y2.astype(BF16)
        else:
            o[0][...] = y.astype(BF16)

    def ep_k(acc, e, o):
        y = _qk_norm(acc, e[0][...], e[1][...])
        o[0][...] = y
        if split_maps:
            y1, y2 = _split_maps(y, functools.partial(_k_aug, heads=heads))
            o[1][...] = y1.astype(BF16)
            o[2][...] = y2.astype(BF16)

    def ep_v(acc, e, o):
        o[0][...] = acc
        if split_maps:
            o[1][...] = acc.astype(BF16)

    maps = [bf_out, bf_out] if split_maps else []
    (u,) = _linear("in_even_u", [h], w_in, 0, wdt, 1, ep_plain, [], [f32_out], tm)
    (zs,) = _linear("in_even_zs", [h], w_in, 1, wdt, 1, ep_silu, [], [f32_out], tm)
    q = _linear("in_even_q", [h], w_in, 2, wdt, 1, ep_q, norm_extras(qg), maps or [bf_out], tm)
    k = _linear("in_even_k", [h], w_in, 3, wdt, 1, ep_k, norm_extras(kg), [f32_out] + maps, tm)
    v = _linear("in_even_v", [h], w_in, 4, wdt, 1, ep_v, [], [f32_out] + maps[:1], tm)
    (za,) = _linear("in_even_za", [h], w_in, 5, wdt, 1, ep_silu, [], [f32_out], tm)
    return u, zs, q, k, v, za


def _glu_proj(gy, w_glu, zs_gate):
    m, wdt = gy.shape
    tm = min(512, m)

    def ep(acc, e, o):
        o[0][...] = (acc[:, :wdt] * jax.nn.sigmoid(acc[:, wdt:]) * e[0][...]).astype(BF16)

    (out,) = _linear("s5_glu", [gy], w_glu, 0, 2 * wdt, 1, ep,
                     [(zs_gate, (tm, wdt), lambda n, i: (i, 0))],
                     [((m, wdt), BF16, (tm, wdt), lambda n, i: (i, 0))])
    return out


def _out_proj(name, xs, w, x_res, mod):
    m, d = x_res.shape
    tm = min(512, m)
    tn = min(1024, d)
    ncb = d // tn
    gate_col = 2 * ncb
    if mod.shape[0] == 1:
        gate = (mod, (1, tn), lambda n, i: (0, gate_col + n))
    else:
        gate = (mod, (tm, tn), lambda n, i: (i, gate_col + n))

    def ep(acc, e, o):
        o[0][...] = e[0][...] + e[1][...] * acc

    (y,) = _linear(name, xs, w, 0, tn, ncb, ep,
                   [(x_res, (tm, tn), lambda n, i: (i, n)), gate],
                   [((m, d), F32, (tm, tn), lambda n, i: (i, n))])
    return y


def _scan_tables(a_r, a_i, coef_s):
    n = a_r.shape[-1]
    row = lax.broadcasted_iota(jnp.int32, (SUBLANES, n), 0)
    a1 = (jnp.broadcast_to(a_r, (SUBLANES, n)), jnp.broadcast_to(a_i, (SUBLANES, n)))
    cmul = lambda x, y: (x[0] * y[0] - x[1] * y[1], x[0] * y[1] + x[1] * y[0])
    a2 = cmul(a1, a1)
    a4 = cmul(a2, a2)
    pc = a1
    for k in range(1, SUBLANES):
        nxt = cmul(pc, a1)
        pc = (jnp.where(row >= k, nxt[0], pc[0]), jnp.where(row >= k, nxt[1], pc[1]))
    for idx, (p, dist) in enumerate(((a1, 1), (a2, 2), (a4, 4))):
        coef_s[2 * idx] = jnp.where(row >= dist, p[0], 0.0)
        coef_s[2 * idx + 1] = jnp.where(row >= dist, p[1], 0.0)
    coef_s[6] = pc[0]
    coef_s[7] = pc[1]


def _scan_tile(a, bb, carry, coefs):
    m1r, m1i, m2r, m2i, m4r, m4i, pcr, pci = coefs
    cr, ci = carry
    for dist, mr, mi in ((1, m1r, m1i), (2, m2r, m2i), (4, m4r, m4i)):
        sa = pltpu.roll(a, dist, 0)
        sb = pltpu.roll(bb, dist, 0)
        a, bb = a + (mr * sa - mi * sb), bb + (mr * sb + mi * sa)
    return a + (pcr * cr - pci * ci), bb + (pcr * ci + pci * cr)


def _s5_block_kernel(u_ref, h0_ref, m_ref, w_ref, v_ref, d_ref, a8r_ref, a8i_ref, gy_ref, hs_ref,
                     x_s, y_s, st_s, coef_s):
    t = pl.program_id(1)
    rows = x_s.shape[0]
    ns = S5_TILE_STATES

    @pl.when(t == 0)
    def _():
        _scan_tables(a8r_ref[...], a8i_ref[...], coef_s)
        st_s[...] = h0_ref[0]

    u8 = jnp.concatenate([u_ref[pl.ds(s, rows, stride=SUBLANES), :] for s in range(SUBLANES)], axis=1)
    ub = u8.astype(BF16)
    x_s[...] = jnp.dot(ub, w_ref[0], preferred_element_type=F32)

    coefs = [coef_s[k] for k in range(8)]
    row = lax.broadcasted_iota(jnp.int32, (SUBLANES, ns), 0)

    def tile(ti, carry):
        r0 = pl.multiple_of(ti * SUBLANES, SUBLANES)
        a, bb = _scan_tile(x_s[pl.ds(r0, SUBLANES), :ns], x_s[pl.ds(r0, SUBLANES), ns:], carry, coefs)
        x_s[pl.ds(r0, SUBLANES), :ns] = jnp.where(row == 0, carry[0], pltpu.roll(a, 1, 0))
        x_s[pl.ds(r0, SUBLANES), ns:] = jnp.where(row == 0, carry[1], pltpu.roll(bb, 1, 0))
        return a[SUBLANES - 1:SUBLANES, :], bb[SUBLANES - 1:SUBLANES, :]

    cr, ci = lax.fori_loop(0, rows // SUBLANES, tile, (st_s[:, :ns], st_s[:, ns:]))
    st_s[:, :ns] = cr
    st_s[:, ns:] = ci

    y8 = (jnp.dot(ub, m_ref[0], preferred_element_type=F32)
          + jnp.dot(x_s[...].astype(BF16), v_ref[0], preferred_element_type=F32))
    for s in range(SUBLANES):
        ls_ = slice(s * LANES, (s + 1) * LANES)
        y_s[pl.ds(s, rows, stride=SUBLANES), :] = y8[:, ls_] + d_ref[...] * u8[:, ls_]
    gy_ref[...] = jax.nn.gelu(y_s[...]).astype(BF16)

    @pl.when(t == pl.num_programs(1) - 1)
    def _():
        hs_ref[0] = st_s[...]


def _s5_blocked(u, prm):
    lb, wdt = u.shape
    nc = wdt // LANES
    ns = S5_TILE_STATES
    rows = math.gcd(256, lb // SUBLANES)
    nt = lb // (rows * SUBLANES)
    tok = pl.BlockSpec((rows * SUBLANES, LANES), lambda c, t: (t, c))
    op = lambda a: pl.BlockSpec((1,) + a.shape[1:], lambda c, t: (c, 0, 0))
    vec = lambda width: pl.BlockSpec((1, width), lambda c, t: (0, c))
    st = pl.BlockSpec((1, 1, 2 * ns), lambda c, t: (c, 0, 0))
    gy, hs = pl.pallas_call(
        _s5_block_kernel,
        grid=(nc, nt),
        in_specs=[tok, st, op(prm["m"]), op(prm["w"]), op(prm["v"]), vec(LANES), vec(ns), vec(ns)],
        out_specs=[tok, st],
        out_shape=[jax.ShapeDtypeStruct((lb, wdt), BF16), jax.ShapeDtypeStruct((nc, 1, 2 * ns), F32)],
        scratch_shapes=[pltpu.VMEM((rows, 2 * ns), F32), pltpu.VMEM((rows * SUBLANES, LANES), F32),
                        pltpu.VMEM((1, 2 * ns), F32), pltpu.VMEM((8, SUBLANES, ns), F32)],
        compiler_params=_params("arbitrary", "arbitrary"),
        name="s5_blocked",
    )(u, jnp.zeros((nc, 1, 2 * ns), F32), prm["m"], prm["w"], prm["v"], prm["d"], prm["a8r"], prm["a8i"])
    hs = hs.reshape(nc, 2, ns)
    return gy, hs[:, 0].reshape(1, 1, nc * ns), hs[:, 1].reshape(1, 1, nc * ns)


def _s5_kernel(u_ref, h0r_ref, h0i_ref, wbr_ref, wbi_ref, wcr_ref, wci_ref, d_ref, ar_ref, ai_ref,
               gy_ref, hr_ref, hi_ref, xr_s, xi_s, sr_s, si_s, coef_s, *, lane_chunk):
    b, t = pl.program_id(0), pl.program_id(1)
    rows, ns = xr_s.shape
    nc = wbr_ref.shape[0]

    @pl.when((b == 0) & (t == 0))
    def _():
        _scan_tables(ar_ref[...], ai_ref[...], coef_s)

    @pl.when(t == 0)
    def _():
        sr_s[...] = h0r_ref[0]
        si_s[...] = h0i_ref[0]

    ub = u_ref[...].astype(BF16)
    for c in range(nc):
        uc = ub[:, c * LANES:(c + 1) * LANES]
        cs = slice(c * S5_TILE_STATES, (c + 1) * S5_TILE_STATES)
        xr_s[:, cs] = jnp.dot(uc, wbr_ref[c], preferred_element_type=F32)
        xi_s[:, cs] = jnp.dot(uc, wbi_ref[c], preferred_element_type=F32)

    for lc in range(ns // lane_chunk):
        ls = slice(lc * lane_chunk, (lc + 1) * lane_chunk)
        m1r, m1i, m2r, m2i, m4r, m4i, pcr, pci = [coef_s[k, :, ls] for k in range(8)]

        def tile(ti, carry, ls=ls, m1r=m1r, m1i=m1i, m2r=m2r, m2i=m2i, m4r=m4r, m4i=m4i,
                 pcr=pcr, pci=pci):
            cr, ci = carry
            r0 = pl.multiple_of(ti * SUBLANES, SUBLANES)
            a = xr_s[pl.ds(r0, SUBLANES), ls]
            bb = xi_s[pl.ds(r0, SUBLANES), ls]
            for dist, mr, mi in ((1, m1r, m1i), (2, m2r, m2i), (4, m4r, m4i)):
                sa = pltpu.roll(a, dist, 0)
                sb = pltpu.roll(bb, dist, 0)
                a, bb = a + (mr * sa - mi * sb), bb + (mr * sb + mi * sa)
            a, bb = a + (pcr * cr - pci * ci), bb + (pcr * ci + pci * cr)
            xr_s[pl.ds(r0, SUBLANES), ls] = a
            xi_s[pl.ds(r0, SUBLANES), ls] = bb
            return a[SUBLANES - 1:SUBLANES, :], bb[SUBLANES - 1:SUBLANES, :]

        cr, ci = lax.fori_loop(0, rows // SUBLANES, tile, (sr_s[:, ls], si_s[:, ls]))
        sr_s[:, ls] = cr
        si_s[:, ls] = ci

    for c in range(nc):
        cs = slice(c * S5_TILE_STATES, (c + 1) * S5_TILE_STATES)
        os_ = slice(c * LANES, (c + 1) * LANES)
        y = (jnp.dot(xr_s[:, cs].astype(BF16), wcr_ref[c], preferred_element_type=F32)
             - jnp.dot(xi_s[:, cs].astype(BF16), wci_ref[c], preferred_element_type=F32)
             + d_ref[:, os_] * u_ref[:, os_])
        gy_ref[:, os_] = jax.nn.gelu(y).astype(BF16)

    @pl.when(t == pl.num_programs(1) - 1)
    def _():
        hr_ref[0] = sr_s[...]
        hi_ref[0] = si_s[...]


def _s5(u, h0r, h0i, prm, nb, lb):
    wdt = u.shape[1]
    ns = h0r.shape[-1]
    nc = wdt // LANES
    rows = min(256, lb)
    nt = lb // rows
    full = lambda a: pl.BlockSpec(a.shape, lambda b, t, nd=a.ndim: (0,) * nd)
    st = pl.BlockSpec((1, 1, ns), lambda b, t: (b, 0, 0))
    tok = pl.BlockSpec((rows, wdt), lambda b, t: (b * nt + t, 0))
    return pl.pallas_call(
        functools.partial(_s5_kernel, lane_chunk=min(512, ns)),
        grid=(nb, nt),
        in_specs=[tok, st, st] + [full(prm[k]) for k in ("wbr", "wbi", "wcr", "wci", "d", "ar", "ai")],
        out_specs=[tok, st, st],
        out_shape=[jax.ShapeDtypeStruct((nb * lb, wdt), BF16),
                   jax.ShapeDtypeStruct((nb, 1, ns), F32),
                   jax.ShapeDtypeStruct((nb, 1, ns), F32)],
        scratch_shapes=[pltpu.VMEM((rows, ns), F32), pltpu.VMEM((rows, ns), F32),
                        pltpu.VMEM((1, ns), F32), pltpu.VMEM((1, ns), F32),
                        pltpu.VMEM((8, SUBLANES, ns), F32)],
        compiler_params=_params("arbitrary", "arbitrary"),
        name="s5_scan",
    )(u, h0r, h0i, prm["wbr"], prm["wbi"], prm["wcr"], prm["wci"], prm["d"], prm["ar"], prm["ai"])


def _s5_params(lam_re, lam_im, log_dt, b_re, b_im, c_re, c_im, d_skip):
    g, p = lam_re.shape
    nc = g // S5_LANE_GROUPS
    dt = jnp.exp(log_dt)[:, None]
    mag = jnp.exp(lam_re * dt)
    ar = mag * jnp.cos(lam_im * dt)
    ai = mag * jnp.sin(lam_im * dt)
    den = lam_re * lam_re + lam_im * lam_im
    zr = ((ar - 1.0) * lam_re + ai * lam_im) / den
    zi = (ai * lam_re - (ar - 1.0) * lam_im) / den
    bbr = zr[..., None] * b_re - zi[..., None] * b_im
    bbi = zr[..., None] * b_im + zi[..., None] * b_re
    eye = jnp.eye(S5_LANE_GROUPS, dtype=F32)

    def bdiag_in(x):
        t = x.reshape(nc, S5_LANE_GROUPS, p, S5_GROUP).transpose(0, 1, 3, 2)
        return (t[:, :, :, None, :] * eye[None, :, None, :, None]).reshape(nc, LANES, S5_TILE_STATES).astype(BF16)

    def bdiag_out(x):
        t = x.reshape(nc, S5_LANE_GROUPS, S5_GROUP, p).transpose(0, 1, 3, 2)
        return (t[:, :, :, None, :] * eye[None, :, None, :, None]).reshape(nc, S5_TILE_STATES, LANES).astype(BF16)

    prm = dict(wbr=bdiag_in(bbr), wbi=bdiag_in(bbi), wcr=bdiag_out(c_re), wci=bdiag_out(c_im),
               d=d_skip.reshape(1, g * S5_GROUP), ar=ar.reshape(1, g * p), ai=ai.reshape(1, g * p))
    prm.update(_s5_block_operators(lam_re * dt, lam_im * dt, bbr, bbi, c_re, c_im))
    return prm


def _s5_block_operators(log_mag, angle, bbr, bbi, c_re, c_im):
    g, p = log_mag.shape
    nc = g // S5_LANE_GROUPS
    t8 = SUBLANES
    hp = lax.Precision.HIGHEST
    j = jnp.arange(t8 + 1, dtype=F32)[:, None, None]
    pw_mag = jnp.exp(j * log_mag[None])
    pw_r, pw_i = pw_mag * jnp.cos(j * angle[None]), pw_mag * jnp.sin(j * angle[None])

    def c_times(lo):
        a_r, a_i = pw_r[lo:lo + t8, :, None, :], pw_i[lo:lo + t8, :, None, :]
        return c_re[None] * a_r - c_im[None] * a_i, c_re[None] * a_i + c_im[None] * a_r

    def block_diagonal(compact, row_group, lane_group):
        rows = compact.shape[1]
        width = S5_LANE_GROUPS * LANES
        lane = jnp.arange(width)
        src = jnp.arange(LANES)
        same = ((lane // (S5_LANE_GROUPS * lane_group))[None, :] == (src // lane_group)[:, None]) \
            & ((lane % lane_group)[None, :] == (src % lane_group)[:, None])
        spread = jnp.einsum("crl,lw->crw", compact.astype(BF16), same.astype(BF16),
                            preferred_element_type=F32)
        row_i = (jnp.arange(rows) // row_group) % S5_LANE_GROUPS
        lane_j = (lane // lane_group) % S5_LANE_GROUPS
        return jnp.where(row_i[:, None] == lane_j[None, :], spread, 0.0).astype(BF16)

    e_r, e_i = c_times(0)
    k = (jnp.einsum("jgop,gpi->jgoi", e_r, bbr, precision=hp)
         - jnp.einsum("jgop,gpi->jgoi", e_i, bbi, precision=hp))
    k_pad = jnp.concatenate([jnp.zeros((t8 - 1,) + k.shape[1:], F32), k], axis=0)
    toe = jnp.stack([k_pad[t8 - 1 - s:2 * t8 - 1 - s] for s in range(t8)])
    toe = toe.reshape(t8, t8, nc, S5_LANE_GROUPS, S5_GROUP, S5_GROUP)
    m = block_diagonal(toe.transpose(2, 0, 3, 5, 1, 4).reshape(nc, t8 * LANES, LANES),
                       S5_GROUP, S5_GROUP)

    j_rev = (t8 - 1) - j[:t8]
    rev_mag = jnp.exp(j_rev * log_mag[None])
    rev_r, rev_i = rev_mag * jnp.cos(j_rev * angle[None]), rev_mag * jnp.sin(j_rev * angle[None])
    ab_r = rev_r[..., None] * bbr[None] - rev_i[..., None] * bbi[None]
    ab_i = rev_r[..., None] * bbi[None] + rev_i[..., None] * bbr[None]
    ab = jnp.stack([ab_r, ab_i]).reshape(2, t8, nc, S5_LANE_GROUPS, p, S5_GROUP)
    w = block_diagonal(ab.transpose(2, 1, 3, 5, 0, 4).reshape(nc, t8 * LANES, 2 * p), S5_GROUP, p)

    f_r, f_i = c_times(1)
    cf = jnp.stack([f_r, -f_i]).reshape(2, t8, nc, S5_LANE_GROUPS, S5_GROUP, p)
    v = block_diagonal(cf.transpose(2, 0, 3, 5, 1, 4).reshape(nc, 2 * S5_TILE_STATES, LANES),
                       p, S5_GROUP)

    return dict(m=m, w=w, v=v, a8r=pw_r[t8].reshape(1, g * p), a8i=pw_i[t8].reshape(1, g * p))


def _stack_maps(q):
    lane = lax.broadcasted_iota(jnp.int32, q.shape, 1)
    zero = jnp.zeros_like(q)
    return jnp.concatenate([jnp.where(lane < QK_DIM, q, zero), jnp.where(lane >= QK_DIM, q, zero)], axis=0)


def _qk(qq, kb):
    return lax.dot_general(qq, kb, (((1,), (1,)), ((), ())), preferred_element_type=F32)


def _attn_finish(o, g_row, gate, lam_init):
    ms = jnp.mean(o * o, axis=-1, keepdims=True)
    return (o * lax.rsqrt(ms + EPS) * g_row * (1.0 - lam_init) * gate).astype(BF16)


def _attn_kernel(slopes_ref, lam_ref, q1_ref, q2_ref, k1_ref, k2_ref, vt_ref, za_ref, og_ref, o_ref,
                 m_s, l_s, acc_s, sa_s, sb_s, ma_s, mb_s, dbias_s, *, lam_init):
    h, i = pl.program_id(0), pl.program_id(1)
    t = q1_ref.shape[0]
    nq = 2 * t
    slope = slopes_ref[h]

    @pl.when(i == 0)
    def _():
        k_local = lax.broadcasted_iota(jnp.int32, (t, nq), 0)
        q_col = lax.broadcasted_iota(jnp.int32, (t, nq), 1)
        q_local = jnp.where(q_col >= t, q_col - t, q_col)
        vis = (k_local // CHUNK) <= (q_local // CHUNK)
        dist = jnp.abs(q_local - k_local).astype(F32)
        dbias_s[...] = (jnp.where(vis, -slope * dist, NEG_INF)
                        + slope * (q_local - k_local).astype(F32))

    q1 = q1_ref[...]
    q2 = q2_ref[...]
    m_s[...] = jnp.full(m_s.shape, -jnp.inf, F32)
    l_s[...] = jnp.zeros(l_s.shape, F32)
    acc_s[...] = jnp.zeros(acc_s.shape, F32)

    def scores(blk, diagonal, s_ref, mx_ref):
        r0 = pl.multiple_of(blk * t, t)
        for half, (k_ref, q) in enumerate(((k1_ref, q1), (k2_ref, q2))):
            cols = slice(half * t, (half + 1) * t)
            s_t = _qk(k_ref[pl.ds(r0, t), :], q)
            if diagonal:
                s_t = s_t + dbias_s[:, cols]
            s_ref[:, cols] = s_t
            mx_ref[:, cols] = jnp.max(s_t, axis=0, keepdims=True)

    def consume(blk, s_ref, mx_ref):
        r0 = pl.multiple_of(blk * t, t)
        shift = slope * ((blk - i) * t).astype(F32)
        m_old = m_s[...]
        m_new = jnp.maximum(m_old, mx_ref[...] + shift)
        alpha = jnp.exp2(m_old - m_new)
        p = jnp.exp2(s_ref[...] - (m_new - shift))
        l_s[...] = alpha * l_s[...] + jnp.sum(p, axis=0, keepdims=True)
        acc_s[...] = alpha * acc_s[...] + jnp.dot(vt_ref[:, pl.ds(r0, t)], p.astype(BF16),
                                                  preferred_element_type=F32)
        m_s[...] = m_new

    scores(i, True, sa_s, ma_s)

    def two_blocks(j0):
        scores(j0, False, sb_s, mb_s)
        consume(jnp.where(j0 == 0, i, j0 - 1), sa_s, ma_s)
        scores(j0 + 1, False, sa_s, ma_s)
        consume(j0, sb_s, mb_s)

    def quad(jj, carry):
        two_blocks(4 * jj)
        two_blocks(4 * jj + 2)
        return carry

    quads = i // 4
    lax.fori_loop(0, quads, quad, 0)

    @pl.when(i % 4 >= 2)
    def _():
        two_blocks(4 * quads)

    paired = i - i % 2
    in_a = jnp.where(paired == 0, i, paired - 1)

    @pl.when(i % 2 == 1)
    def _():
        scores(i - 1, False, sb_s, mb_s)
        consume(in_a, sa_s, ma_s)
        consume(i - 1, sb_s, mb_s)

    @pl.when(i % 2 == 0)
    def _():
        consume(in_a, sa_s, ma_s)

    acc = acc_s[...]
    l = l_s[...]
    o_t = acc[:, :t] / l[:, :t] - lam_ref[0] * (acc[:, t:] / l[:, t:])
    ms = jnp.mean(o_t * o_t, axis=0, keepdims=True)
    y = (o_t * lax.rsqrt(ms + EPS)).T
    o_ref[...] = (y * og_ref[...] * (1.0 - lam_init) * za_ref[...]).astype(BF16)


def _attention_prompt(q1, q2, k1, k2, vb, za_gate, out_g, lam, lam_init):
    m, wdt = q1.shape
    heads = wdt // HEAD_DIM
    t = min(ATT_BLOCK, m)
    slopes = LOG2E * 2.0 ** (-8.0 * jnp.arange(1, heads + 1, dtype=F32) / heads)
    smem = pl.BlockSpec(memory_space=pltpu.SMEM)
    blk = pl.BlockSpec((t, HEAD_DIM), lambda h, i: (i, h))
    keys = pl.BlockSpec((m, HEAD_DIM), lambda h, i: (0, h))
    return pl.pallas_call(
        functools.partial(_attn_kernel, lam_init=lam_init),
        grid=(heads, m // t),
        in_specs=[smem, smem, blk, blk, keys, keys,
                  pl.BlockSpec((HEAD_DIM, m), lambda h, i: (h, 0)),
                  blk, pl.BlockSpec((1, HEAD_DIM), lambda h, i: (0, 0))],
        out_specs=blk,
        out_shape=jax.ShapeDtypeStruct((m, wdt), BF16),
        scratch_shapes=[pltpu.VMEM((1, 2 * t), F32), pltpu.VMEM((1, 2 * t), F32),
                        pltpu.VMEM((HEAD_DIM, 2 * t), F32),
                        pltpu.VMEM((t, 2 * t), F32), pltpu.VMEM((t, 2 * t), F32),
                        pltpu.VMEM((1, 2 * t), F32), pltpu.VMEM((1, 2 * t), F32),
                        pltpu.VMEM((t, 2 * t), F32)],
        compiler_params=_params("arbitrary", "arbitrary"),
        name="attn_prompt",
    )(slopes, lam, q1, q2, k1, k2, vb.T, za_gate, out_g.reshape(1, HEAD_DIM))


MASKED_DIST = 1e32


def _attn_cached_kernel(lam_ref, q_ref, kn_ref, vn_ref, kc_ref, vc_ref, za_ref, og_ref, o_ref,
                        dc_s, dn_s, *, lam_init, heads):
    ls = q_ref.shape[0]
    past = kc_ref.shape[2]
    lam = lam_ref[0]

    @pl.when(pl.program_id(0) == 0)
    def _():
        def table(n, k0):
            q_row = lax.broadcasted_iota(jnp.int32, (2 * ls, n), 0)
            q_pos = past + jnp.where(q_row >= ls, q_row - ls, q_row)
            k_pos = k0 + lax.broadcasted_iota(jnp.int32, (2 * ls, n), 1)
            vis = (k_pos // CHUNK) <= (q_pos // CHUNK)
            return jnp.where(vis, jnp.abs(q_pos - k_pos).astype(F32), MASKED_DIST)

        dc_s[...] = table(past, 0)
        dn_s[...] = table(ls, past)

    for h in range(heads):
        hs = slice(h * HEAD_DIM, (h + 1) * HEAD_DIM)
        slope = LOG2E * 2.0 ** (-8.0 * (h + 1) / heads)
        qq = _stack_maps(q_ref[:, hs])
        sc = _qk(qq, kc_ref[0, h]) - slope * dc_s[...]
        sn = _qk(qq, kn_ref[:, hs].astype(BF16)) - slope * dn_s[...]
        mx = jnp.maximum(jnp.max(sc, axis=1, keepdims=True), jnp.max(sn, axis=1, keepdims=True))
        pc = jnp.exp2(sc - mx)
        pn = jnp.exp2(sn - mx)
        inv = 1.0 / (jnp.sum(pc, axis=1, keepdims=True) + jnp.sum(pn, axis=1, keepdims=True))
        pc = pc * inv
        pn = pn * inv
        wc = (pc[:ls] - lam * pc[ls:]).astype(BF16)
        wn = (pn[:ls] - lam * pn[ls:]).astype(BF16)
        o = (jnp.dot(wc, vc_ref[0, h], preferred_element_type=F32)
             + jnp.dot(wn, vn_ref[:, hs].astype(BF16), preferred_element_type=F32))
        o_ref[:, hs] = _attn_finish(o, og_ref[...], za_ref[:, hs], lam_init)


def _attention_cached(q, k_new, v_new, cache_k, cache_v, za_gate, out_g, lam, lam_init, nb, ls):
    wdt = q.shape[1]
    heads = wdt // HEAD_DIM
    past = cache_k.shape[1]
    head_major = lambda a: jnp.transpose(a.astype(BF16), (0, 2, 1, 3))
    tok = pl.BlockSpec((ls, wdt), lambda b: (b, 0))
    cache = pl.BlockSpec((1, heads, past, HEAD_DIM), lambda b: (b, 0, 0, 0))
    return pl.pallas_call(
        functools.partial(_attn_cached_kernel, lam_init=lam_init, heads=heads),
        grid=(nb,),
        in_specs=[pl.BlockSpec(memory_space=pltpu.SMEM), tok, tok, tok, cache, cache, tok,
                  pl.BlockSpec((1, HEAD_DIM), lambda b: (0, 0))],
        out_specs=tok,
        out_shape=jax.ShapeDtypeStruct((nb * ls, wdt), BF16),
        scratch_shapes=[pltpu.VMEM((2 * ls, past), F32), pltpu.VMEM((2 * ls, ls), F32)],
        compiler_params=_params("arbitrary"),
        name="attn_cached",
    )(lam, q, k_new, v_new, head_major(cache_k), head_major(cache_v), za_gate,
      out_g.reshape(1, HEAD_DIM))


def _odd_in_proj(h, w_in):
    m, d = h.shape
    tm = min(512, m)
    tn = min(512, d)
    ncb = d // tn

    def kernel(h_ref, wa_ref, wb_ref, wz_ref, g_ref, z_ref, w_s):
        @pl.when(pl.program_id(1) == 0)
        def _():
            for k, w_ref in enumerate((wa_ref, wb_ref, wz_ref)):
                w_s[k] = w_ref[...].astype(BF16)

        x = h_ref[...]
        a = jnp.dot(x, w_s[0], preferred_element_type=F32)
        b = jnp.dot(x, w_s[1], preferred_element_type=F32)
        z = jnp.dot(x, w_s[2], preferred_element_type=F32)
        g_ref[...] = a * jax.nn.sigmoid(b)
        z_ref[...] = _silu(z)

    wspec = lambda off: pl.BlockSpec((d, tn), lambda n, i, off=off: (0, n + off))
    out = pl.BlockSpec((tm, tn), lambda n, i: (i, n))
    return pl.pallas_call(
        kernel,
        grid=(ncb, m // tm),
        in_specs=[pl.BlockSpec((tm, d), lambda n, i: (i, 0)), wspec(0), wspec(ncb), wspec(2 * ncb)],
        out_specs=[out, out],
        out_shape=[jax.ShapeDtypeStruct((m, d), F32), jax.ShapeDtypeStruct((m, d), F32)],
        scratch_shapes=[pltpu.VMEM((3, d, tn), BF16)],
        compiler_params=_params("arbitrary", "arbitrary"),
        name="in_odd",
    )(h, w_in, w_in, w_in)


CONV_HALO = 32


LN_ROWS = 64


def _conv_kernel(cur_ref, prev_ref, ctx_ref, zg_ref, w_ref, b_ref, lg_ref, lb_ref, o_ref, xp_s, y_s,
                 sh_s, *, row_chunk):
    t = pl.program_id(1)
    tm, c = cur_ref.shape

    @pl.when(t == 0)
    def _():
        xp_s[0:CONV_HALO, :] = ctx_ref[0]

    @pl.when(t > 0)
    def _():
        xp_s[0:CONV_HALO, :] = prev_ref[...]

    xp_s[CONV_HALO:CONV_HALO + tm, :] = cur_ref[...]
    xp_s[CONV_HALO + tm:, :] = jnp.zeros((SUBLANES, c), F32)

    first = CONV_HALO - (CONV_W - 1)
    span = row_chunk + CONV_HALO

    def lane_body(lc, carry):
        c0 = pl.multiple_of(lc * LANES, LANES)

        def row_body(rc, carry2):
            r0 = pl.multiple_of(rc * row_chunk, row_chunk)
            blk = xp_s[pl.ds(r0, span + SUBLANES), pl.ds(c0, LANES)]
            for phase in range(SUBLANES):
                sh_s[phase] = blk[phase:phase + span]
            acc = jnp.zeros((row_chunk, LANES), F32)
            for phase in range(SUBLANES):
                for a in range(span // SUBLANES):
                    tap = SUBLANES * a + phase - first
                    if 0 <= tap < CONV_W:
                        acc = acc + (w_ref[pl.ds(tap, 1), pl.ds(c0, LANES)]
                                     * sh_s[phase, SUBLANES * a:SUBLANES * a + row_chunk, :])
            y_s[pl.ds(r0, row_chunk), pl.ds(c0, LANES)] = acc
            return carry2

        return lax.fori_loop(0, tm // row_chunk, row_body, carry)

    lax.fori_loop(0, c // LANES, lane_body, 0)

    ln_rows = min(LN_ROWS, tm)

    def ln_body(r, carry):
        rows = pl.ds(pl.multiple_of(r * ln_rows, ln_rows), ln_rows)
        y = y_s[rows, :] + b_ref[...]
        mu = jnp.mean(y, axis=-1, keepdims=True)
        yc = y - mu
        var = jnp.mean(yc * yc, axis=-1, keepdims=True)
        yn = yc * lax.rsqrt(var + EPS) * lg_ref[...] + lb_ref[...]
        o_ref[rows, :] = (_silu(yn) * zg_ref[rows, :]).astype(BF16)
        return carry

    lax.fori_loop(0, tm // ln_rows, ln_body, 0)


def _conv_module(g, ctx, zg, conv_w, conv_b, ln_g, ln_b, nb, lb):
    c = g.shape[1]
    tm = min(256, lb)
    nt = lb // tm
    halo_blocks = tm // CONV_HALO
    w_pad = jnp.zeros((CONV_HALO, c), F32).at[:CONV_W].set(conv_w)
    cur = pl.BlockSpec((tm, c), lambda b, t: (b * nt + t, 0))
    prev = pl.BlockSpec((CONV_HALO, c), lambda b, t: (jnp.maximum((b * nt + t) * halo_blocks - 1, 0), 0))
    vec = pl.BlockSpec((1, c), lambda b, t: (0, 0))
    return pl.pallas_call(
        functools.partial(_conv_kernel, row_chunk=min(64, tm)),
        grid=(nb, nt),
        in_specs=[cur, prev, pl.BlockSpec((1, CONV_HALO, c), lambda b, t: (b, 0, 0)), cur,
                  pl.BlockSpec((CONV_HALO, c), lambda b, t: (0, 0)), vec, vec, vec],
        out_specs=cur,
        out_shape=jax.ShapeDtypeStruct((nb * lb, c), BF16),
        scratch_shapes=[pltpu.VMEM((CONV_HALO + tm + SUBLANES, c), F32), pltpu.VMEM((tm, c), F32),
                        pltpu.VMEM((SUBLANES, min(64, tm) + CONV_HALO, LANES), F32)],
        compiler_params=_params("arbitrary", "arbitrary"),
        name="conv_module",
    )(g, g, ctx, zg, w_pad, conv_b.reshape(1, c), ln_g.reshape(1, c), ln_b.reshape(1, c))


def kernel(x_prompt, x_sample, c_prompt, c_sample, cache_k, cache_v, state_s5_re, state_s5_im, state_conv, norm_g, w_ada, b_ada, w_in_even, w_out_even, s5_lam_re, s5_lam_im, s5_log_dt, s5_b_re, s5_b_im, s5_c_re, s5_c_im, s5_d, s5_w_glu, q_norm_g, k_norm_g, lam_q1, lam_k1, lam_q2, lam_k2, attn_out_g, w_in_odd, conv_w, conv_b, conv_ln_g, conv_ln_b, w_out_odd):
    bp, lp, d = x_prompt.shape
    bs, ls, _ = x_sample.shape
    assert bp == 1 and w_ada.shape[0] == 2, "one prompt sequence, one even and one odd layer"
    wdt = d // 2
    heads = wdt // HEAD_DIM
    ns = (wdt // S5_GROUP) * S5_STATE
    ms = bs * ls

    xp = x_prompt.reshape(lp, d)
    xs = x_sample.reshape(ms, d)

    rows = bp + bs
    rows_pad = -(-rows // 16) * 16
    c_all = jnp.concatenate([c_prompt, c_sample, jnp.zeros((rows_pad - rows, d), F32)], axis=0)
    mod = _ada_mod(c_all, w_ada, b_ada)
    mod_p = [mod[l, :bp] for l in range(2)]
    mod_s = [jnp.repeat(mod[l, bp:rows], ls, axis=0) for l in range(2)]

    w_in_e, w_out_e, w_glu = w_in_even[0], w_out_even[0], s5_w_glu[0]
    w_in_o, w_out_o = w_in_odd[0], w_out_odd[0]

    s5p = _s5_params(s5_lam_re[0], s5_lam_im[0], s5_log_dt[0], s5_b_re[0], s5_b_im[0],
                     s5_c_re[0], s5_c_im[0], s5_d[0])
    lam_init = 0.8 - 0.6 * math.exp(-0.3 * 0)
    lam = (jnp.exp(jnp.sum(lam_q1[0] * lam_k1[0])) - jnp.exp(jnp.sum(lam_q2[0] * lam_k2[0]))
           + lam_init).reshape(1).astype(F32)

    def even_layer(x, modl, nb, lb, h0r, h0i, cache):
        h = _modnorm(x, modl, norm_g[0])
        u, zs, q, k, v, za = _even_in_proj(h, w_in_e, q_norm_g[0], k_norm_g[0], cache is None)
        gy, hr, hi = _s5_blocked(u, s5p) if cache is None else _s5(u, h0r, h0i, s5p, nb, lb)
        s5_out = _glu_proj(gy, w_glu, zs)
        if cache is None:
            att = _attention_prompt(q[0], q[1], k[1], k[2], v[1], za, attn_out_g[0], lam, lam_init)
        else:
            att = _attention_cached(q[0], k[0], v[0], cache[0], cache[1], za, attn_out_g[0], lam,
                                    lam_init, nb, lb)
        y = _out_proj("out_even", [s5_out, att], w_out_e, x, modl)
        return y, k[0], v[0], hr, hi

    def odd_layer(x, modl, nb, lb, ctx):
        h = _modnorm(x, modl, norm_g[1])
        g, zg = _odd_in_proj(h, w_in_o)
        yc = _conv_module(g, ctx, zg, conv_w[0], conv_b[0], conv_ln_g[0], conv_ln_b[0], nb, lb)
        y = _out_proj("out_odd", [yc], w_out_o, x, modl)
        return y, g

    zero_state = jnp.zeros((bp, 1, ns), F32)
    yp, kp, vp, hrp, hip = even_layer(xp, mod_p[0], bp, lp, zero_state, zero_state, None)
    past = cache_k.shape[2]
    ys, ksn, vsn, hrs, his = even_layer(
        xs, mod_s[0], bs, ls, state_s5_re[0].reshape(bs, 1, ns), state_s5_im[0].reshape(bs, 1, ns),
        (cache_k[0], cache_v[0]))

    pad = CONV_HALO - (CONV_W - 1)
    ctx_p = jnp.zeros((bp, CONV_HALO, d), F32)
    ctx_s = jnp.pad(state_conv[0], ((0, 0), (pad, 0), (0, 0)))
    yp, gp = odd_layer(yp, mod_p[1], bp, lp, ctx_p)
    ys, gs = odd_layer(ys, mod_s[1], bs, ls, ctx_s)

    groups = wdt // S5_GROUP
    tail = CONV_W - 1
    conv_p = gp.reshape(bp, lp, d)[:, lp - tail:]
    conv_s = jnp.concatenate([state_conv[0], gs.reshape(bs, ls, d)], axis=1)[:, -tail:]
    return (yp.reshape(bp, lp, d), ys.reshape(bs, ls, d),
            kp.reshape(1, bp, lp, heads, HEAD_DIM), vp.reshape(1, bp, lp, heads, HEAD_DIM),
            hrp.reshape(1, bp, groups, S5_STATE), hip.reshape(1, bp, groups, S5_STATE), conv_p[None],
            ksn.reshape(1, bs, ls, heads, HEAD_DIM), vsn.reshape(1, bs, ls, heads, HEAD_DIM),
            hrs.reshape(1, bs, groups, S5_STATE), his.reshape(1, bs, groups, S5_STATE), conv_s[None])
```

```python
import functools
import math

import jax
import jax.numpy as jnp
import numpy as np
from jax import lax
from jax.experimental import pallas as pl
from jax.experimental.pallas import tpu as pltpu

F32 = jnp.float32
BF16 = jnp.bfloat16

CHUNK = 64
HEAD_DIM = 128
QK_DIM = 64
S5_GROUP = 16
S5_STATE = 64
CONV_W = 31
EPS = 1e-6
NEG_INF = -1e30
LOG2E = 1.4426950408889634

LANES = 128
SUBLANES = 8
ATT_BLOCK = 512
S5_STEP_BLOCKS = 1024
S5_LANE_GROUPS = LANES // S5_GROUP
S5_TILE_STATES = S5_LANE_GROUPS * S5_STATE
VMEM_LIMIT = 48 * 1024 * 1024


def _params(*sem):
    return pltpu.CompilerParams(dimension_semantics=sem, vmem_limit_bytes=VMEM_LIMIT)


def _silu(x):
    return x * jax.nn.sigmoid(x)


def _ada_kernel(c_ref, w_ref, b_ref, o_ref):
    a = _silu(c_ref[...]).astype(BF16)
    w = w_ref[0].astype(BF16)
    o_ref[0] = jnp.dot(a, w, preferred_element_type=F32) + b_ref[0]


def _ada_mod(c_all, w_ada, b_ada):
    depth, d, n = w_ada.shape
    rows = c_all.shape[0]
    tn = 512
    return pl.pallas_call(
        _ada_kernel,
        grid=(depth, n // tn),
        in_specs=[pl.BlockSpec((rows, d), lambda l, j: (0, 0)),
                  pl.BlockSpec((1, d, tn), lambda l, j: (l, 0, j)),
                  pl.BlockSpec((1, 1, tn), lambda l, j: (l, 0, j))],
        out_specs=pl.BlockSpec((1, rows, tn), lambda l, j: (l, 0, j)),
        out_shape=jax.ShapeDtypeStruct((depth, rows, n), F32),
        compiler_params=_params("arbitrary", "arbitrary"),
        name="ada_mod",
    )(c_all, w_ada, b_ada.reshape(depth, 1, n))


def _modnorm_kernel(x_ref, sh_ref, sc_ref, g_ref, o_ref):
    x = x_ref[...]
    ms = jnp.mean(x * x, axis=-1, keepdims=True)
    y = x * lax.rsqrt(ms + EPS) * g_ref[...]
    o_ref[...] = (y * (1.0 + sc_ref[...]) + sh_ref[...]).astype(BF16)


def _row_spec(arr, tm, width, col_block):
    if arr.shape[0] == 1:
        return pl.BlockSpec((1, width), lambda i, c=col_block: (0, c))
    return pl.BlockSpec((tm, width), lambda i, c=col_block: (i, c))


def _modnorm(x, mod, g):
    m, d = x.shape
    tm = min(512, m)
    return pl.pallas_call(
        _modnorm_kernel,
        grid=(m // tm,),
        in_specs=[pl.BlockSpec((tm, d), lambda i: (i, 0)),
                  _row_spec(mod, tm, d, 0),
                  _row_spec(mod, tm, d, 1),
                  pl.BlockSpec((1, d), lambda i: (0, 0))],
        out_specs=pl.BlockSpec((tm, d), lambda i: (i, 0)),
        out_shape=jax.ShapeDtypeStruct((m, d), BF16),
        compiler_params=_params("arbitrary"),
        name="modnorm",
    )(x, mod, mod, g.reshape(1, d))


def _linear(name, xs, w, col0, ncols, ncb, epilogue, extras, outs, tm=512):
    m = xs[0].shape[0]
    tm = min(tm, m)
    k_total = w.shape[0]
    nx, ne = len(xs), len(extras)

    def kernel(*refs):
        x_refs, w_ref = refs[:nx], refs[nx]
        e_refs, o_refs, wb_s = refs[nx + 1:nx + 1 + ne], refs[nx + 1 + ne:-1], refs[-1]

        @pl.when(pl.program_id(1) == 0)
        def _():
            wb_s[...] = w_ref[...].astype(BF16)

        acc, k0 = None, 0
        for xr in x_refs:
            kp = xr.shape[1]
            part = jnp.dot(xr[...], wb_s[k0:k0 + kp, :], preferred_element_type=F32)
            acc = part if acc is None else acc + part
            k0 += kp
        epilogue(acc, e_refs, o_refs)

    in_specs = [pl.BlockSpec((tm, x.shape[1]), lambda n, i: (i, 0)) for x in xs]
    in_specs.append(pl.BlockSpec((k_total, ncols), lambda n, i: (0, col0 + n)))
    in_specs += [pl.BlockSpec(bs, im) for _, bs, im in extras]
    res = pl.pallas_call(
        kernel,
        grid=(ncb, m // tm),
        in_specs=in_specs,
        out_specs=[pl.BlockSpec(bs, im) for _, _, bs, im in outs],
        out_shape=[jax.ShapeDtypeStruct(s, dt) for s, dt, _, _ in outs],
        scratch_shapes=[pltpu.VMEM((k_total, ncols), BF16)],
        compiler_params=_params("arbitrary", "arbitrary"),
        name=name,
    )(*xs, w, *[a for a, _, _ in extras])
    return res


def _half_mean_matrix():
    r = lax.broadcasted_iota(jnp.int32, (HEAD_DIM, HEAD_DIM), 0) // QK_DIM
    c = lax.broadcasted_iota(jnp.int32, (HEAD_DIM, HEAD_DIM), 1) // QK_DIM
    return jnp.where(r == c, 1.0 / QK_DIM, 0.0).astype(BF16)


def _qk_norm(acc, g_row, s_mat):
    heads = acc.shape[1] // HEAD_DIM
    outs = []
    for h in range(heads):
        xh = acc[:, h * HEAD_DIM:(h + 1) * HEAD_DIM]
        t = xh * xh
        hi = t.astype(BF16)
        lo = (t - hi.astype(F32)).astype(BF16)
        ms = (jnp.dot(hi, s_mat, preferred_element_type=F32)
              + jnp.dot(lo, s_mat, preferred_element_type=F32))
        outs.append(xh * lax.rsqrt(ms + EPS))
    return jnp.concatenate(outs, axis=1) * g_row


ALIBI_LANES = 3


def _split_maps(y, aug):
    tm = y.shape[0]
    lane = lax.broadcasted_iota(jnp.int32, (tm, HEAD_DIM), 1)
    first, second = [], []
    for h in range(y.shape[1] // HEAD_DIM):
        hs = slice(h * HEAD_DIM, (h + 1) * HEAD_DIM)
        first.append(jnp.where(lane < QK_DIM, y[:, hs], aug[:, hs]))
        second.append(jnp.where(lane < QK_DIM, pltpu.roll(y[:, hs], QK_DIM, 1), aug[:, hs]))
    return jnp.concatenate(first, axis=1), jnp.concatenate(second, axis=1)


def _aug_tables(tm, heads):
    q_tab = np.zeros((tm, heads, HEAD_DIM), np.float32)
    q_tab[:, :, QK_DIM:QK_DIM + ALIBI_LANES] = 1.0
    k_tab = np.zeros((tm, heads, HEAD_DIM), np.float32)
    slope = LOG2E * 2.0 ** (-8.0 * np.arange(1, heads + 1, dtype=np.float32) / heads)
    rest = (np.arange(tm, dtype=np.float32)[:, None] * slope[None, :]).astype(np.float32)
    for term in range(ALIBI_LANES):
        part = rest.astype(jnp.bfloat16).astype(np.float32)
        k_tab[:, :, QK_DIM + term] = part
        rest = rest - part
    return q_tab.reshape(tm, heads * HEAD_DIM), k_tab.reshape(tm, heads * HEAD_DIM)


def _even_in_proj(h, w_in, q_g, k_g, split_maps):
    m = h.shape[0]
    wdt = w_in.shape[1] // 6
    heads = wdt // HEAD_DIM
    tm = min(ATT_BLOCK, m)
    s_mat = _half_mean_matrix()
    row = lambda n, i: (i, 0)
    fix = lambda n, i: (0, 0)
    f32_out = ((m, wdt), F32, (tm, wdt), row)
    bf_out = ((m, wdt), BF16, (tm, wdt), row)
    qg = jnp.tile(q_g, heads).reshape(1, wdt)
    kg = jnp.tile(k_g, heads).reshape(1, wdt)
    q_tab, k_tab = _aug_tables(tm, heads)
    norm_extras = lambda g, tab: ([(g, (1, wdt), fix), (s_mat, (HEAD_DIM, HEAD_DIM), fix)]
                                  + ([(jnp.asarray(tab), (tm, wdt), fix)] if split_maps else []))

    def ep_plain(acc, e, o):
        o[0][...] = acc

    def ep_silu(acc, e, o):
        o[0][...] = _silu(acc)

    def ep_q(acc, e, o):
        y = _qk_norm(acc, e[0][...], e[1][...]) * (LOG2E * QK_DIM ** -0.5)
        if split_maps:
            y1, y2 = _split_maps(y, e[2][...])
            o[0][...] = y1.astype(BF16)
            o[1][...] = y2.astype(BF16)
        else:
            o[0][...] = y.astype(BF16)

    def ep_k(acc, e, o):
        y = _qk_norm(acc, e[0][...], e[1][...])
        o[0][...] = y
        if split_maps:
            y1, y2 = _split_maps(y, e[2][...])
            o[1][...] = y1.astype(BF16)
            o[2][...] = y2.astype(BF16)

    def ep_v(acc, e, o):
        o[0][...] = acc
        if split_maps:
            o[1][...] = acc.astype(BF16)

    maps = [bf_out, bf_out] if split_maps else []
    (u,) = _linear("in_even_u", [h], w_in, 0, wdt, 1, ep_plain, [], [f32_out], tm)
    (zs,) = _linear("in_even_zs", [h], w_in, 1, wdt, 1, ep_silu, [], [f32_out], tm)
    q = _linear("in_even_q", [h], w_in, 2, wdt, 1, ep_q, norm_extras(qg, q_tab), maps or [bf_out], tm)
    k = _linear("in_even_k", [h], w_in, 3, wdt, 1, ep_k, norm_extras(kg, k_tab), [f32_out] + maps, tm)
    v = _linear("in_even_v", [h], w_in, 4, wdt, 1, ep_v, [], [f32_out] + maps[:1], tm)
    (za,) = _linear("in_even_za", [h], w_in, 5, wdt, 1, ep_silu, [], [f32_out], tm)
    return u, zs, q, k, v, za


def _glu_proj(gy, w_glu, zs_gate):
    m, wdt = gy.shape
    tm = min(512, m)

    def ep(acc, e, o):
        o[0][...] = (acc[:, :wdt] * jax.nn.sigmoid(acc[:, wdt:]) * e[0][...]).astype(BF16)

    (out,) = _linear("s5_glu", [gy], w_glu, 0, 2 * wdt, 1, ep,
                     [(zs_gate, (tm, wdt), lambda n, i: (i, 0))],
                     [((m, wdt), BF16, (tm, wdt), lambda n, i: (i, 0))])
    return out


def _out_proj(name, xs, w, x_res, mod):
    m, d = x_res.shape
    tm = min(512, m)
    tn = min(1024, d)
    ncb = d // tn
    gate_col = 2 * ncb
    if mod.shape[0] == 1:
        gate = (mod, (1, tn), lambda n, i: (0, gate_col + n))
    else:
        gate = (mod, (tm, tn), lambda n, i: (i, gate_col + n))

    def ep(acc, e, o):
        o[0][...] = e[0][...] + e[1][...] * acc

    (y,) = _linear(name, xs, w, 0, tn, ncb, ep,
                   [(x_res, (tm, tn), lambda n, i: (i, n)), gate],
                   [((m, d), F32, (tm, tn), lambda n, i: (i, n))])
    return y


def _scan_tables(a_r, a_i, coef_s):
    n = a_r.shape[-1]
    row = lax.broadcasted_iota(jnp.int32, (SUBLANES, n), 0)
    a1 = (jnp.broadcast_to(a_r, (SUBLANES, n)), jnp.broadcast_to(a_i, (SUBLANES, n)))
    cmul = lambda x, y: (x[0] * y[0] - x[1] * y[1], x[0] * y[1] + x[1] * y[0])
    a2 = cmul(a1, a1)
    a4 = cmul(a2, a2)
    pc = a1
    for k in range(1, SUBLANES):
        nxt = cmul(pc, a1)
        pc = (jnp.where(row >= k, nxt[0], pc[0]), jnp.where(row >= k, nxt[1], pc[1]))
    for idx, (p, dist) in enumerate(((a1, 1), (a2, 2), (a4, 4))):
        coef_s[2 * idx] = jnp.where(row >= dist, p[0], 0.0)
        coef_s[2 * idx + 1] = jnp.where(row >= dist, p[1], 0.0)
    coef_s[6] = pc[0]
    coef_s[7] = pc[1]


def _scan_tile(a, bb, carry, coefs):
    m1r, m1i, m2r, m2i, m4r, m4i, pcr, pci = coefs
    cr, ci = carry
    for dist, mr, mi in ((1, m1r, m1i), (2, m2r, m2i), (4, m4r, m4i)):
        sa = pltpu.roll(a, dist, 0)
        sb = pltpu.roll(bb, dist, 0)
        a, bb = a + (mr * sa - mi * sb), bb + (mr * sb + mi * sa)
    return a + (pcr * cr - pci * ci), bb + (pcr * ci + pci * cr)


def _s5_block_kernel(u_ref, h0_ref, m_ref, w_ref, v_ref, d_ref, a8r_ref, a8i_ref, gy_ref, hs_ref,
                     x_s, y_s, st_s, coef_s):
    t = pl.program_id(1)
    rows = x_s.shape[0]
    ns = S5_TILE_STATES

    @pl.when(t == 0)
    def _():
        _scan_tables(a8r_ref[...], a8i_ref[...], coef_s)
        st_s[...] = h0_ref[0]

    u8 = jnp.concatenate([u_ref[pl.ds(s, rows, stride=SUBLANES), :] for s in range(SUBLANES)], axis=1)
    ub = u8.astype(BF16)
    x_s[...] = jnp.dot(ub, w_ref[0], preferred_element_type=F32)

    coefs = [coef_s[k] for k in range(8)]
    row = lax.broadcasted_iota(jnp.int32, (SUBLANES, ns), 0)

    def tile(ti, carry):
        r0 = pl.multiple_of(ti * SUBLANES, SUBLANES)
        a, bb = _scan_tile(x_s[pl.ds(r0, SUBLANES), :ns], x_s[pl.ds(r0, SUBLANES), ns:], carry, coefs)
        x_s[pl.ds(r0, SUBLANES), :ns] = jnp.where(row == 0, carry[0], pltpu.roll(a, 1, 0))
        x_s[pl.ds(r0, SUBLANES), ns:] = jnp.where(row == 0, carry[1], pltpu.roll(bb, 1, 0))
        return a[SUBLANES - 1:SUBLANES, :], bb[SUBLANES - 1:SUBLANES, :]

    cr, ci = lax.fori_loop(0, rows // SUBLANES, tile, (st_s[:, :ns], st_s[:, ns:]))
    st_s[:, :ns] = cr
    st_s[:, ns:] = ci

    y8 = (jnp.dot(ub, m_ref[0], preferred_element_type=F32)
          + jnp.dot(x_s[...].astype(BF16), v_ref[0], preferred_element_type=F32))
    for s in range(SUBLANES):
        ls_ = slice(s * LANES, (s + 1) * LANES)
        y_s[pl.ds(s, rows, stride=SUBLANES), :] = y8[:, ls_] + d_ref[...] * u8[:, ls_]
    gy_ref[...] = jax.nn.gelu(y_s[...]).astype(BF16)

    @pl.when(t == pl.num_programs(1) - 1)
    def _():
        hs_ref[0] = st_s[...]


def _s5_blocked(u, prm):
    lb, wdt = u.shape
    nc = wdt // LANES
    ns = S5_TILE_STATES
    rows = math.gcd(S5_STEP_BLOCKS, lb // SUBLANES)
    nt = lb // (rows * SUBLANES)
    tok = pl.BlockSpec((rows * SUBLANES, LANES), lambda c, t: (t, c))
    op = lambda a: pl.BlockSpec((1,) + a.shape[1:], lambda c, t: (c, 0, 0))
    vec = lambda width: pl.BlockSpec((1, width), lambda c, t: (0, c))
    st = pl.BlockSpec((1, 1, 2 * ns), lambda c, t: (c, 0, 0))
    gy, hs = pl.pallas_call(
        _s5_block_kernel,
        grid=(nc, nt),
        in_specs=[tok, st, op(prm["m"]), op(prm["w"]), op(prm["v"]), vec(LANES), vec(ns), vec(ns)],
        out_specs=[tok, st],
        out_shape=[jax.ShapeDtypeStruct((lb, wdt), BF16), jax.ShapeDtypeStruct((nc, 1, 2 * ns), F32)],
        scratch_shapes=[pltpu.VMEM((rows, 2 * ns), F32), pltpu.VMEM((rows * SUBLANES, LANES), F32),
                        pltpu.VMEM((1, 2 * ns), F32), pltpu.VMEM((8, SUBLANES, ns), F32)],
        compiler_params=_params("arbitrary", "arbitrary"),
        name="s5_blocked",
    )(u, jnp.zeros((nc, 1, 2 * ns), F32), prm["m"], prm["w"], prm["v"], prm["d"], prm["a8r"], prm["a8i"])
    hs = hs.reshape(nc, 2, ns)
    return gy, hs[:, 0].reshape(1, 1, nc * ns), hs[:, 1].reshape(1, 1, nc * ns)


def _s5_kernel(u_ref, h0r_ref, h0i_ref, wbr_ref, wbi_ref, wcr_ref, wci_ref, d_ref, ar_ref, ai_ref,
               gy_ref, hr_ref, hi_ref, xr_s, xi_s, sr_s, si_s, coef_s, *, lane_chunk):
    b, t = pl.program_id(0), pl.program_id(1)
    rows, ns = xr_s.shape
    nc = wbr_ref.shape[0]

    @pl.when((b == 0) & (t == 0))
    def _():
        _scan_tables(ar_ref[...], ai_ref[...], coef_s)

    @pl.when(t == 0)
    def _():
        sr_s[...] = h0r_ref[0]
        si_s[...] = h0i_ref[0]

    ub = u_ref[...].astype(BF16)
    for c in range(nc):
        uc = ub[:, c * LANES:(c + 1) * LANES]
        cs = slice(c * S5_TILE_STATES, (c + 1) * S5_TILE_STATES)
        xr_s[:, cs] = jnp.dot(uc, wbr_ref[c], preferred_element_type=F32)
        xi_s[:, cs] = jnp.dot(uc, wbi_ref[c], preferred_element_type=F32)

    for lc in range(ns // lane_chunk):
        ls = slice(lc * lane_chunk, (lc + 1) * lane_chunk)
        coefs = [coef_s[k, :, ls] for k in range(8)]

        def tile(ti, carry, ls=ls, coefs=coefs):
            r0 = pl.multiple_of(ti * SUBLANES, SUBLANES)
            a, bb = _scan_tile(xr_s[pl.ds(r0, SUBLANES), ls], xi_s[pl.ds(r0, SUBLANES), ls], carry, coefs)
            xr_s[pl.ds(r0, SUBLANES), ls] = a
            xi_s[pl.ds(r0, SUBLANES), ls] = bb
            return a[SUBLANES - 1:SUBLANES, :], bb[SUBLANES - 1:SUBLANES, :]

        cr, ci = lax.fori_loop(0, rows // SUBLANES, tile, (sr_s[:, ls], si_s[:, ls]))
        sr_s[:, ls] = cr
        si_s[:, ls] = ci

    for c in range(nc):
        cs = slice(c * S5_TILE_STATES, (c + 1) * S5_TILE_STATES)
        os_ = slice(c * LANES, (c + 1) * LANES)
        y = (jnp.dot(xr_s[:, cs].astype(BF16), wcr_ref[c], preferred_element_type=F32)
             - jnp.dot(xi_s[:, cs].astype(BF16), wci_ref[c], preferred_element_type=F32)
             + d_ref[:, os_] * u_ref[:, os_])
        gy_ref[:, os_] = jax.nn.gelu(y).astype(BF16)

    @pl.when(t == pl.num_programs(1) - 1)
    def _():
        hr_ref[0] = sr_s[...]
        hi_ref[0] = si_s[...]


def _s5(u, h0r, h0i, prm, nb, lb):
    wdt = u.shape[1]
    ns = h0r.shape[-1]
    nc = wdt // LANES
    rows = min(256, lb)
    nt = lb // rows
    full = lambda a: pl.BlockSpec(a.shape, lambda b, t, nd=a.ndim: (0,) * nd)
    st = pl.BlockSpec((1, 1, ns), lambda b, t: (b, 0, 0))
    tok = pl.BlockSpec((rows, wdt), lambda b, t: (b * nt + t, 0))
    return pl.pallas_call(
        functools.partial(_s5_kernel, lane_chunk=min(512, ns)),
        grid=(nb, nt),
        in_specs=[tok, st, st] + [full(prm[k]) for k in ("wbr", "wbi", "wcr", "wci", "d", "ar", "ai")],
        out_specs=[tok, st, st],
        out_shape=[jax.ShapeDtypeStruct((nb * lb, wdt), BF16),
                   jax.ShapeDtypeStruct((nb, 1, ns), F32),
                   jax.ShapeDtypeStruct((nb, 1, ns), F32)],
        scratch_shapes=[pltpu.VMEM((rows, ns), F32), pltpu.VMEM((rows, ns), F32),
                        pltpu.VMEM((1, ns), F32), pltpu.VMEM((1, ns), F32),
                        pltpu.VMEM((8, SUBLANES, ns), F32)],
        compiler_params=_params("arbitrary", "arbitrary"),
        name="s5_scan",
    )(u, h0r, h0i, prm["wbr"], prm["wbi"], prm["wcr"], prm["wci"], prm["d"], prm["ar"], prm["ai"])


def _s5_params(lam_re, lam_im, log_dt, b_re, b_im, c_re, c_im, d_skip):
    g, p = lam_re.shape
    nc = g // S5_LANE_GROUPS
    dt = jnp.exp(log_dt)[:, None]
    mag = jnp.exp(lam_re * dt)
    ar = mag * jnp.cos(lam_im * dt)
    ai = mag * jnp.sin(lam_im * dt)
    den = lam_re * lam_re + lam_im * lam_im
    zr = ((ar - 1.0) * lam_re + ai * lam_im) / den
    zi = (ai * lam_re - (ar - 1.0) * lam_im) / den
    bbr = zr[..., None] * b_re - zi[..., None] * b_im
    bbi = zr[..., None] * b_im + zi[..., None] * b_re
    eye = jnp.eye(S5_LANE_GROUPS, dtype=F32)

    def bdiag_in(x):
        t = x.reshape(nc, S5_LANE_GROUPS, p, S5_GROUP).transpose(0, 1, 3, 2)
        return (t[:, :, :, None, :] * eye[None, :, None, :, None]).reshape(nc, LANES, S5_TILE_STATES).astype(BF16)

    def bdiag_out(x):
        t = x.reshape(nc, S5_LANE_GROUPS, S5_GROUP, p).transpose(0, 1, 3, 2)
        return (t[:, :, :, None, :] * eye[None, :, None, :, None]).reshape(nc, S5_TILE_STATES, LANES).astype(BF16)

    prm = dict(wbr=bdiag_in(bbr), wbi=bdiag_in(bbi), wcr=bdiag_out(c_re), wci=bdiag_out(c_im),
               d=d_skip.reshape(1, g * S5_GROUP), ar=ar.reshape(1, g * p), ai=ai.reshape(1, g * p))
    prm.update(_s5_block_operators(lam_re * dt, lam_im * dt, bbr, bbi, c_re, c_im))
    return prm


def _s5_block_operators(log_mag, angle, bbr, bbi, c_re, c_im):
    g, p = log_mag.shape
    nc = g // S5_LANE_GROUPS
    t8 = SUBLANES
    hp = lax.Precision.HIGHEST
    j = jnp.arange(t8 + 1, dtype=F32)[:, None, None]
    pw_mag = jnp.exp(j * log_mag[None])
    pw_r, pw_i = pw_mag * jnp.cos(j * angle[None]), pw_mag * jnp.sin(j * angle[None])

    def c_times(lo):
        a_r, a_i = pw_r[lo:lo + t8, :, None, :], pw_i[lo:lo + t8, :, None, :]
        return c_re[None] * a_r - c_im[None] * a_i, c_re[None] * a_i + c_im[None] * a_r

    def block_diagonal(compact, row_group, lane_group):
        rows = compact.shape[1]
        width = S5_LANE_GROUPS * LANES
        lane = jnp.arange(width)
        src = jnp.arange(LANES)
        same = ((lane // (S5_LANE_GROUPS * lane_group))[None, :] == (src // lane_group)[:, None]) \
            & ((lane % lane_group)[None, :] == (src % lane_group)[:, None])
        spread = jnp.einsum("crl,lw->crw", compact.astype(BF16), same.astype(BF16),
                            preferred_element_type=F32)
        row_i = (jnp.arange(rows) // row_group) % S5_LANE_GROUPS
        lane_j = (lane // lane_group) % S5_LANE_GROUPS
        return jnp.where(row_i[:, None] == lane_j[None, :], spread, 0.0).astype(BF16)

    e_r, e_i = c_times(0)
    k = (jnp.einsum("jgop,gpi->jgoi", e_r, bbr, precision=hp)
         - jnp.einsum("jgop,gpi->jgoi", e_i, bbi, precision=hp))
    k_pad = jnp.concatenate([jnp.zeros((t8 - 1,) + k.shape[1:], F32), k], axis=0)
    toe = jnp.stack([k_pad[t8 - 1 - s:2 * t8 - 1 - s] for s in range(t8)])
    toe = toe.reshape(t8, t8, nc, S5_LANE_GROUPS, S5_GROUP, S5_GROUP)
    m = block_diagonal(toe.transpose(2, 0, 3, 5, 1, 4).reshape(nc, t8 * LANES, LANES),
                       S5_GROUP, S5_GROUP)

    j_rev = (t8 - 1) - j[:t8]
    rev_mag = jnp.exp(j_rev * log_mag[None])
    rev_r, rev_i = rev_mag * jnp.cos(j_rev * angle[None]), rev_mag * jnp.sin(j_rev * angle[None])
    ab_r = rev_r[..., None] * bbr[None] - rev_i[..., None] * bbi[None]
    ab_i = rev_r[..., None] * bbi[None] + rev_i[..., None] * bbr[None]
    ab = jnp.stack([ab_r, ab_i]).reshape(2, t8, nc, S5_LANE_GROUPS, p, S5_GROUP)
    w = block_diagonal(ab.transpose(2, 1, 3, 5, 0, 4).reshape(nc, t8 * LANES, 2 * p), S5_GROUP, p)

    f_r, f_i = c_times(1)
    cf = jnp.stack([f_r, -f_i]).reshape(2, t8, nc, S5_LANE_GROUPS, S5_GROUP, p)
    v = block_diagonal(cf.transpose(2, 0, 3, 5, 1, 4).reshape(nc, 2 * S5_TILE_STATES, LANES),
                       p, S5_GROUP)

    return dict(m=m, w=w, v=v, a8r=pw_r[t8].reshape(1, g * p), a8i=pw_i[t8].reshape(1, g * p))


def _stack_maps(q):
    lane = lax.broadcasted_iota(jnp.int32, q.shape, 1)
    zero = jnp.zeros_like(q)
    return jnp.concatenate([jnp.where(lane < QK_DIM, q, zero), jnp.where(lane >= QK_DIM, q, zero)], axis=0)


def _qk(qq, kb):
    return lax.dot_general(qq, kb, (((1,), (1,)), ((), ())), preferred_element_type=F32)


def _attn_finish(o, g_row, gate, lam_init):
    ms = jnp.mean(o * o, axis=-1, keepdims=True)
    return (o * lax.rsqrt(ms + EPS) * g_row * (1.0 - lam_init) * gate).astype(BF16)


def _attn_kernel(slopes_ref, lam_ref, q1_ref, q2_ref, k1_ref, k2_ref, vt_ref, za_ref, og_ref, o_ref,
                 m_s, l_s, acc_s, sa_s, sb_s, ma_s, mb_s, dbias_s, *, lam_init):
    h, i = pl.program_id(0), pl.program_id(1)
    t = q1_ref.shape[0]
    nq = 2 * t
    slope = slopes_ref[h]

    @pl.when(i == 0)
    def _():
        k_local = lax.broadcasted_iota(jnp.int32, (t, nq), 0)
        q_col = lax.broadcasted_iota(jnp.int32, (t, nq), 1)
        q_local = jnp.where(q_col >= t, q_col - t, q_col)
        vis = (k_local // CHUNK) <= (q_local // CHUNK)
        dist = jnp.abs(q_local - k_local).astype(F32)
        dbias_s[...] = (jnp.where(vis, -slope * dist, NEG_INF)
                        + slope * (q_local - k_local).astype(F32))

    q1 = q1_ref[...]
    q2 = q2_ref[...]
    m_s[...] = jnp.full(m_s.shape, -jnp.inf, F32)
    l_s[...] = jnp.zeros(l_s.shape, F32)
    acc_s[...] = jnp.zeros(acc_s.shape, F32)

    def scores(blk, diagonal, s_ref, mx_ref):
        r0 = pl.multiple_of(blk * t, t)
        for half, (k_ref, q) in enumerate(((k1_ref, q1), (k2_ref, q2))):
            cols = slice(half * t, (half + 1) * t)
            s_t = _qk(k_ref[pl.ds(r0, t), :], q)
            if diagonal:
                s_t = s_t + dbias_s[:, cols]
            s_ref[:, cols] = s_t
            mx_ref[:, cols] = jnp.max(s_t, axis=0, keepdims=True)

    def consume(blk, s_ref, mx_ref):
        r0 = pl.multiple_of(blk * t, t)
        shift = slope * ((blk - i) * t).astype(F32)
        m_old = m_s[...]
        m_new = jnp.maximum(m_old, mx_ref[...] + shift)
        alpha = jnp.exp2(m_old - m_new)
        p = jnp.exp2(s_ref[...] - (m_new - shift))
        l_s[...] = alpha * l_s[...] + jnp.sum(p, axis=0, keepdims=True)
        acc_s[...] = alpha * acc_s[...] + jnp.dot(vt_ref[:, pl.ds(r0, t)], p.astype(BF16),
                                                  preferred_element_type=F32)
        m_s[...] = m_new

    scores(i, True, sa_s, ma_s)

    def two_blocks(j0):
        scores(j0, False, sb_s, mb_s)
        consume(jnp.where(j0 == 0, i, j0 - 1), sa_s, ma_s)
        scores(j0 + 1, False, sa_s, ma_s)
        consume(j0, sb_s, mb_s)

    def quad(jj, carry):
        two_blocks(4 * jj)
        two_blocks(4 * jj + 2)
        return carry

    quads = i // 4
    lax.fori_loop(0, quads, quad, 0)

    @pl.when(i % 4 >= 2)
    def _():
        two_blocks(4 * quads)

    paired = i - i % 2
    in_a = jnp.where(paired == 0, i, paired - 1)

    @pl.when(i % 2 == 1)
    def _():
        scores(i - 1, False, sb_s, mb_s)
        consume(in_a, sa_s, ma_s)
        consume(i - 1, sb_s, mb_s)

    @pl.when(i % 2 == 0)
    def _():
        consume(in_a, sa_s, ma_s)

    acc = acc_s[...]
    l = l_s[...]
    o_t = acc[:, :t] / l[:, :t] - lam_ref[0] * (acc[:, t:] / l[:, t:])
    ms = jnp.mean(o_t * o_t, axis=0, keepdims=True)
    y = (o_t * lax.rsqrt(ms + EPS)).T
    o_ref[...] = (y * og_ref[...] * (1.0 - lam_init) * za_ref[...]).astype(BF16)


def _attention_prompt(q1, q2, k1, k2, vb, za_gate, out_g, lam, lam_init):
    m, wdt = q1.shape
    heads = wdt // HEAD_DIM
    t = min(ATT_BLOCK, m)
    slopes = LOG2E * 2.0 ** (-8.0 * jnp.arange(1, heads + 1, dtype=F32) / heads)
    smem = pl.BlockSpec(memory_space=pltpu.SMEM)
    blk = pl.BlockSpec((t, HEAD_DIM), lambda h, i: (i, h))
    keys = pl.BlockSpec((m, HEAD_DIM), lambda h, i: (0, h))
    return pl.pallas_call(
        functools.partial(_attn_kernel, lam_init=lam_init),
        grid=(heads, m // t),
        in_specs=[smem, smem, blk, blk, keys, keys,
                  pl.BlockSpec((HEAD_DIM, m), lambda h, i: (h, 0)),
                  blk, pl.BlockSpec((1, HEAD_DIM), lambda h, i: (0, 0))],
        out_specs=blk,
        out_shape=jax.ShapeDtypeStruct((m, wdt), BF16),
        scratch_shapes=[pltpu.VMEM((1, 2 * t), F32), pltpu.VMEM((1, 2 * t), F32),
                        pltpu.VMEM((HEAD_DIM, 2 * t), F32),
                        pltpu.VMEM((t, 2 * t), F32), pltpu.VMEM((t, 2 * t), F32),
                        pltpu.VMEM((1, 2 * t), F32), pltpu.VMEM((1, 2 * t), F32),
                        pltpu.VMEM((t, 2 * t), F32)],
        compiler_params=_params("arbitrary", "arbitrary"),
        name="attn_prompt",
    )(slopes, lam, q1, q2, k1, k2, vb.T, za_gate, out_g.reshape(1, HEAD_DIM))


MASKED_DIST = 1e32


def _attn_cached_kernel(lam_ref, q_ref, kn_ref, vn_ref, kc_ref, vc_ref, za_ref, og_ref, o_ref,
                        dc_s, dn_s, *, lam_init, heads):
    ls = q_ref.shape[0]
    past = kc_ref.shape[2]
    lam = lam_ref[0]

    @pl.when(pl.program_id(0) == 0)
    def _():
        def table(n, k0):
            q_row = lax.broadcasted_iota(jnp.int32, (2 * ls, n), 0)
            q_pos = past + jnp.where(q_row >= ls, q_row - ls, q_row)
            k_pos = k0 + lax.broadcasted_iota(jnp.int32, (2 * ls, n), 1)
            vis = (k_pos // CHUNK) <= (q_pos // CHUNK)
            return jnp.where(vis, jnp.abs(q_pos - k_pos).astype(F32), MASKED_DIST)

        dc_s[...] = table(past, 0)
        dn_s[...] = table(ls, past)

    for h in range(heads):
        hs = slice(h * HEAD_DIM, (h + 1) * HEAD_DIM)
        slope = LOG2E * 2.0 ** (-8.0 * (h + 1) / heads)
        qq = _stack_maps(q_ref[:, hs])
        sc = _qk(qq, kc_ref[0, h]) - slope * dc_s[...]
        sn = _qk(qq, kn_ref[:, hs].astype(BF16)) - slope * dn_s[...]
        mx = jnp.maximum(jnp.max(sc, axis=1, keepdims=True), jnp.max(sn, axis=1, keepdims=True))
        pc = jnp.exp2(sc - mx)
        pn = jnp.exp2(sn - mx)
        inv = 1.0 / (jnp.sum(pc, axis=1, keepdims=True) + jnp.sum(pn, axis=1, keepdims=True))
        pc = pc * inv
        pn = pn * inv
        wc = (pc[:ls] - lam * pc[ls:]).astype(BF16)
        wn = (pn[:ls] - lam * pn[ls:]).astype(BF16)
        o = (jnp.dot(wc, vc_ref[0, h], preferred_element_type=F32)
             + jnp.dot(wn, vn_ref[:, hs].astype(BF16), preferred_element_type=F32))
        o_ref[:, hs] = _attn_finish(o, og_ref[...], za_ref[:, hs], lam_init)


def _attention_cached(q, k_new, v_new, cache_k, cache_v, za_gate, out_g, lam, lam_init, nb, ls):
    wdt = q.shape[1]
    heads = wdt // HEAD_DIM
    past = cache_k.shape[1]
    head_major = lambda a: jnp.transpose(a.astype(BF16), (0, 2, 1, 3))
    tok = pl.BlockSpec((ls, wdt), lambda b: (b, 0))
    cache = pl.BlockSpec((1, heads, past, HEAD_DIM), lambda b: (b, 0, 0, 0))
    return pl.pallas_call(
        functools.partial(_attn_cached_kernel, lam_init=lam_init, heads=heads),
        grid=(nb,),
        in_specs=[pl.BlockSpec(memory_space=pltpu.SMEM), tok, tok, tok, cache, cache, tok,
                  pl.BlockSpec((1, HEAD_DIM), lambda b: (0, 0))],
        out_specs=tok,
        out_shape=jax.ShapeDtypeStruct((nb * ls, wdt), BF16),
        scratch_shapes=[pltpu.VMEM((2 * ls, past), F32), pltpu.VMEM((2 * ls, ls), F32)],
        compiler_params=_params("arbitrary"),
        name="attn_cached",
    )(lam, q, k_new, v_new, head_major(cache_k), head_major(cache_v), za_gate,
      out_g.reshape(1, HEAD_DIM))


def _odd_in_proj(h, w_in):
    m, d = h.shape
    tm = min(512, m)
    tn = min(512, d)
    ncb = d // tn

    def kernel(h_ref, wa_ref, wb_ref, wz_ref, g_ref, z_ref, w_s):
        @pl.when(pl.program_id(1) == 0)
        def _():
            for k, w_ref in enumerate((wa_ref, wb_ref, wz_ref)):
                w_s[k] = w_ref[...].astype(BF16)

        x = h_ref[...]
        a = jnp.dot(x, w_s[0], preferred_element_type=F32)
        b = jnp.dot(x, w_s[1], preferred_element_type=F32)
        z = jnp.dot(x, w_s[2], preferred_element_type=F32)
        g_ref[...] = a * jax.nn.sigmoid(b)
        z_ref[...] = _silu(z)

    wspec = lambda off: pl.BlockSpec((d, tn), lambda n, i, off=off: (0, n + off))
    out = pl.BlockSpec((tm, tn), lambda n, i: (i, n))
    return pl.pallas_call(
        kernel,
        grid=(ncb, m // tm),
        in_specs=[pl.BlockSpec((tm, d), lambda n, i: (i, 0)), wspec(0), wspec(ncb), wspec(2 * ncb)],
        out_specs=[out, out],
        out_shape=[jax.ShapeDtypeStruct((m, d), F32), jax.ShapeDtypeStruct((m, d), F32)],
        scratch_shapes=[pltpu.VMEM((3, d, tn), BF16)],
        compiler_params=_params("arbitrary", "arbitrary"),
        name="in_odd",
    )(h, w_in, w_in, w_in)


CONV_HALO = 32


LN_ROWS = 64


def _conv_kernel(cur_ref, prev_ref, ctx_ref, zg_ref, w_ref, b_ref, lg_ref, lb_ref, o_ref, xp_s, y_s,
                 sh_s, *, row_chunk):
    t = pl.program_id(1)
    tm, c = cur_ref.shape

    @pl.when(t == 0)
    def _():
        xp_s[0:CONV_HALO, :] = ctx_ref[0]

    @pl.when(t > 0)
    def _():
        xp_s[0:CONV_HALO, :] = prev_ref[...]

    xp_s[CONV_HALO:CONV_HALO + tm, :] = cur_ref[...]
    xp_s[CONV_HALO + tm:, :] = jnp.zeros((SUBLANES, c), F32)

    first = CONV_HALO - (CONV_W - 1)
    span = row_chunk + CONV_HALO

    def lane_body(lc, carry):
        c0 = pl.multiple_of(lc * LANES, LANES)

        def row_body(rc, carry2):
            r0 = pl.multiple_of(rc * row_chunk, row_chunk)
            blk = xp_s[pl.ds(r0, span + SUBLANES), pl.ds(c0, LANES)]
            for phase in range(SUBLANES):
                sh_s[phase] = blk[phase:phase + span]
            acc = jnp.zeros((row_chunk, LANES), F32)
            for phase in range(SUBLANES):
                for a in range(span // SUBLANES):
                    tap = SUBLANES * a + phase - first
                    if 0 <= tap < CONV_W:
                        acc = acc + (w_ref[pl.ds(tap, 1), pl.ds(c0, LANES)]
                                     * sh_s[phase, SUBLANES * a:SUBLANES * a + row_chunk, :])
            y_s[pl.ds(r0, row_chunk), pl.ds(c0, LANES)] = acc
            return carry2

        return lax.fori_loop(0, tm // row_chunk, row_body, carry)

    lax.fori_loop(0, c // LANES, lane_body, 0)

    ln_rows = min(LN_ROWS, tm)

    def ln_body(r, carry):
        rows = pl.ds(pl.multiple_of(r * ln_rows, ln_rows), ln_rows)
        y = y_s[rows, :] + b_ref[...]
        mu = jnp.mean(y, axis=-1, keepdims=True)
        yc = y - mu
        var = jnp.mean(yc * yc, axis=-1, keepdims=True)
        yn = yc * lax.rsqrt(var + EPS) * lg_ref[...] + lb_ref[...]
        o_ref[rows, :] = (_silu(yn) * zg_ref[rows, :]).astype(BF16)
        return carry

    lax.fori_loop(0, tm // ln_rows, ln_body, 0, unroll=True)


def _conv_module(g, ctx, zg, conv_w, conv_b, ln_g, ln_b, nb, lb):
    c = g.shape[1]
    tm = min(256, lb)
    nt = lb // tm
    halo_blocks = tm // CONV_HALO
    w_pad = jnp.zeros((CONV_HALO, c), F32).at[:CONV_W].set(conv_w)
    cur = pl.BlockSpec((tm, c), lambda b, t: (b * nt + t, 0))
    prev = pl.BlockSpec((CONV_HALO, c), lambda b, t: (jnp.maximum((b * nt + t) * halo_blocks - 1, 0), 0))
    vec = pl.BlockSpec((1, c), lambda b, t: (0, 0))
    return pl.pallas_call(
        functools.partial(_conv_kernel, row_chunk=min(64, tm)),
        grid=(nb, nt),
        in_specs=[cur, prev, pl.BlockSpec((1, CONV_HALO, c), lambda b, t: (b, 0, 0)), cur,
                  pl.BlockSpec((CONV_HALO, c), lambda b, t: (0, 0)), vec, vec, vec],
        out_specs=cur,
        out_shape=jax.ShapeDtypeStruct((nb * lb, c), BF16),
        scratch_shapes=[pltpu.VMEM((CONV_HALO + tm + SUBLANES, c), F32), pltpu.VMEM((tm, c), F32),
                        pltpu.VMEM((SUBLANES, min(64, tm) + CONV_HALO, LANES), F32)],
        compiler_params=_params("arbitrary", "arbitrary"),
        name="conv_module",
    )(g, g, ctx, zg, w_pad, conv_b.reshape(1, c), ln_g.reshape(1, c), ln_b.reshape(1, c))


def kernel(x_prompt, x_sample, c_prompt, c_sample, cache_k, cache_v, state_s5_re, state_s5_im, state_conv, norm_g, w_ada, b_ada, w_in_even, w_out_even, s5_lam_re, s5_lam_im, s5_log_dt, s5_b_re, s5_b_im, s5_c_re, s5_c_im, s5_d, s5_w_glu, q_norm_g, k_norm_g, lam_q1, lam_k1, lam_q2, lam_k2, attn_out_g, w_in_odd, conv_w, conv_b, conv_ln_g, conv_ln_b, w_out_odd):
    bp, lp, d = x_prompt.shape
    bs, ls, _ = x_sample.shape
    assert bp == 1 and w_ada.shape[0] == 2, "one prompt sequence, one even and one odd layer"
    wdt = d // 2
    heads = wdt // HEAD_DIM
    ns = (wdt // S5_GROUP) * S5_STATE
    ms = bs * ls

    xp = x_prompt.reshape(lp, d)
    xs = x_sample.reshape(ms, d)

    rows = bp + bs
    rows_pad = -(-rows // 16) * 16
    c_all = jnp.concatenate([c_prompt, c_sample, jnp.zeros((rows_pad - rows, d), F32)], axis=0)
    mod = _ada_mod(c_all, w_ada, b_ada)
    mod_p = [mod[l, :bp] for l in range(2)]
    mod_s = [jnp.repeat(mod[l, bp:rows], ls, axis=0) for l in range(2)]

    w_in_e, w_out_e, w_glu = w_in_even[0], w_out_even[0], s5_w_glu[0]
    w_in_o, w_out_o = w_in_odd[0], w_out_odd[0]

    s5p = _s5_params(s5_lam_re[0], s5_lam_im[0], s5_log_dt[0], s5_b_re[0], s5_b_im[0],
                     s5_c_re[0], s5_c_im[0], s5_d[0])
    lam_init = 0.8 - 0.6 * math.exp(-0.3 * 0)
    lam = (jnp.exp(jnp.sum(lam_q1[0] * lam_k1[0])) - jnp.exp(jnp.sum(lam_q2[0] * lam_k2[0]))
           + lam_init).reshape(1).astype(F32)

    def even_layer(x, modl, nb, lb, h0r, h0i, cache):
        h = _modnorm(x, modl, norm_g[0])
        u, zs, q, k, v, za = _even_in_proj(h, w_in_e, q_norm_g[0], k_norm_g[0], cache is None)
        gy, hr, hi = _s5_blocked(u, s5p) if cache is None else _s5(u, h0r, h0i, s5p, nb, lb)
        s5_out = _glu_proj(gy, w_glu, zs)
        if cache is None:
            att = _attention_prompt(q[0], q[1], k[1], k[2], v[1], za, attn_out_g[0], lam, lam_init)
        else:
            att = _attention_cached(q[0], k[0], v[0], cache[0], cache[1], za, attn_out_g[0], lam,
                                    lam_init, nb, lb)
        y = _out_proj("out_even", [s5_out, att], w_out_e, x, modl)
        return y, k[0], v[0], hr, hi

    def odd_layer(x, modl, nb, lb, ctx):
        h = _modnorm(x, modl, norm_g[1])
        g, zg = _odd_in_proj(h, w_in_o)
        yc = _conv_module(g, ctx, zg, conv_w[0], conv_b[0], conv_ln_g[0], conv_ln_b[0], nb, lb)
        y = _out_proj("out_odd", [yc], w_out_o, x, modl)
        return y, g

    zero_state = jnp.zeros((bp, 1, ns), F32)
    yp, kp, vp, hrp, hip = even_layer(xp, mod_p[0], bp, lp, zero_state, zero_state, None)
    ys, ksn, vsn, hrs, his = even_layer(
        xs, mod_s[0], bs, ls, state_s5_re[0].reshape(bs, 1, ns), state_s5_im[0].reshape(bs, 1, ns),
        (cache_k[0], cache_v[0]))

    pad = CONV_HALO - (CONV_W - 1)
    ctx_p = jnp.zeros((bp, CONV_HALO, d), F32)
    ctx_s = jnp.pad(state_conv[0], ((0, 0), (pad, 0), (0, 0)))
    yp, gp = odd_layer(yp, mod_p[1], bp, lp, ctx_p)
    ys, gs = odd_layer(ys, mod_s[1], bs, ls, ctx_s)

    groups = wdt // S5_GROUP
    tail = CONV_W - 1
    conv_p = gp.reshape(bp, lp, d)[:, lp - tail:]
    conv_s = jnp.concatenate([state_conv[0], gs.reshape(bs, ls, d)], axis=1)[:, -tail:]
    return (yp.reshape(bp, lp, d), ys.reshape(bs, ls, d),
            kp.reshape(1, bp, lp, heads, HEAD_DIM), vp.reshape(1, bp, lp, heads, HEAD_DIM),
            hrp.reshape(1, bp, groups, S5_STATE), hip.reshape(1, bp, groups, S5_STATE), conv_p[None],
            ksn.reshape(1, bs, ls, heads, HEAD_DIM), vsn.reshape(1, bs, ls, heads, HEAD_DIM),
            hrs.reshape(1, bs, groups, S5_STATE), his.reshape(1, bs, groups, S5_STATE), conv_s[None])
```

```python
import functools
import math

import jax
import jax.numpy as jnp
import numpy as np
from jax import lax
from jax.experimental import pallas as pl
from jax.experimental.pallas import tpu as pltpu

F32 = jnp.float32
BF16 = jnp.bfloat16

CHUNK = 64
HEAD_DIM = 128
QK_DIM = 64
S5_GROUP = 16
S5_STATE = 64
CONV_W = 31
EPS = 1e-6
NEG_INF = -1e30
LOG2E = 1.4426950408889634

LANES = 128
SUBLANES = 8
ATT_BLOCK = 512
S5_STEP_BLOCKS = 1024
S5_LANE_GROUPS = LANES // S5_GROUP
S5_TILE_STATES = S5_LANE_GROUPS * S5_STATE
VMEM_LIMIT = 48 * 1024 * 1024


def _params(*sem):
    return pltpu.CompilerParams(dimension_semantics=sem, vmem_limit_bytes=VMEM_LIMIT)


def _silu(x):
    return x * jax.nn.sigmoid(x)


def _ada_kernel(c_ref, w_ref, b_ref, o_ref):
    a = _silu(c_ref[...]).astype(BF16)
    w = w_ref[0].astype(BF16)
    o_ref[0] = jnp.dot(a, w, preferred_element_type=F32) + b_ref[0]


def _ada_mod(c_all, w_ada, b_ada):
    depth, d, n = w_ada.shape
    rows = c_all.shape[0]
    tn = 512
    return pl.pallas_call(
        _ada_kernel,
        grid=(depth, n // tn),
        in_specs=[pl.BlockSpec((rows, d), lambda l, j: (0, 0)),
                  pl.BlockSpec((1, d, tn), lambda l, j: (l, 0, j)),
                  pl.BlockSpec((1, 1, tn), lambda l, j: (l, 0, j))],
        out_specs=pl.BlockSpec((1, rows, tn), lambda l, j: (l, 0, j)),
        out_shape=jax.ShapeDtypeStruct((depth, rows, n), F32),
        compiler_params=_params("arbitrary", "arbitrary"),
        name="ada_mod",
    )(c_all, w_ada, b_ada.reshape(depth, 1, n))


def _modnorm_kernel(x_ref, sh_ref, sc_ref, g_ref, o_ref):
    x = x_ref[...]
    ms = jnp.mean(x * x, axis=-1, keepdims=True)
    y = x * lax.rsqrt(ms + EPS) * g_ref[...]
    o_ref[...] = (y * (1.0 + sc_ref[...]) + sh_ref[...]).astype(BF16)


def _row_spec(arr, tm, width, col_block):
    if arr.shape[0] == 1:
        return pl.BlockSpec((1, width), lambda i, c=col_block: (0, c))
    return pl.BlockSpec((tm, width), lambda i, c=col_block: (i, c))


def _modnorm(x, mod, g):
    m, d = x.shape
    tm = min(512, m)
    return pl.pallas_call(
        _modnorm_kernel,
        grid=(m // tm,),
        in_specs=[pl.BlockSpec((tm, d), lambda i: (i, 0)),
                  _row_spec(mod, tm, d, 0),
                  _row_spec(mod, tm, d, 1),
                  pl.BlockSpec((1, d), lambda i: (0, 0))],
        out_specs=pl.BlockSpec((tm, d), lambda i: (i, 0)),
        out_shape=jax.ShapeDtypeStruct((m, d), BF16),
        compiler_params=_params("arbitrary"),
        name="modnorm",
    )(x, mod, mod, g.reshape(1, d))


def _linear(name, xs, w, col0, ncols, ncb, epilogue, extras, outs, tm=512):
    m = xs[0].shape[0]
    tm = min(tm, m)
    k_total = w.shape[0]
    nx, ne = len(xs), len(extras)

    def kernel(*refs):
        x_refs, w_ref = refs[:nx], refs[nx]
        e_refs, o_refs, wb_s = refs[nx + 1:nx + 1 + ne], refs[nx + 1 + ne:-1], refs[-1]

        @pl.when(pl.program_id(1) == 0)
        def _():
            wb_s[...] = w_ref[...].astype(BF16)

        acc, k0 = None, 0
        for xr in x_refs:
            kp = xr.shape[1]
            part = jnp.dot(xr[...], wb_s[k0:k0 + kp, :], preferred_element_type=F32)
            acc = part if acc is None else acc + part
            k0 += kp
        epilogue(acc, e_refs, o_refs)

    in_specs = [pl.BlockSpec((tm, x.shape[1]), lambda n, i: (i, 0)) for x in xs]
    in_specs.append(pl.BlockSpec((k_total, ncols), lambda n, i: (0, col0 + n)))
    in_specs += [pl.BlockSpec(bs, im) for _, bs, im in extras]
    res = pl.pallas_call(
        kernel,
        grid=(ncb, m // tm),
        in_specs=in_specs,
        out_specs=[pl.BlockSpec(bs, im) for _, _, bs, im in outs],
        out_shape=[jax.ShapeDtypeStruct(s, dt) for s, dt, _, _ in outs],
        scratch_shapes=[pltpu.VMEM((k_total, ncols), BF16)],
        compiler_params=_params("arbitrary", "arbitrary"),
        name=name,
    )(*xs, w, *[a for a, _, _ in extras])
    return res


def _half_mean_matrix():
    r = lax.broadcasted_iota(jnp.int32, (HEAD_DIM, HEAD_DIM), 0) // QK_DIM
    c = lax.broadcasted_iota(jnp.int32, (HEAD_DIM, HEAD_DIM), 1) // QK_DIM
    return jnp.where(r == c, 1.0 / QK_DIM, 0.0).astype(BF16)


def _qk_norm(acc, g_row, s_mat):
    heads = acc.shape[1] // HEAD_DIM
    outs = []
    for h in range(heads):
        xh = acc[:, h * HEAD_DIM:(h + 1) * HEAD_DIM]
        t = xh * xh
        hi = t.astype(BF16)
        lo = (t - hi.astype(F32)).astype(BF16)
        ms = (jnp.dot(hi, s_mat, preferred_element_type=F32)
              + jnp.dot(lo, s_mat, preferred_element_type=F32))
        outs.append(xh * lax.rsqrt(ms + EPS))
    return jnp.concatenate(outs, axis=1) * g_row


ALIBI_LANES = 3


def _split_maps(y, aug):
    tm = y.shape[0]
    lane = lax.broadcasted_iota(jnp.int32, (tm, HEAD_DIM), 1)
    first, second = [], []
    for h in range(y.shape[1] // HEAD_DIM):
        hs = slice(h * HEAD_DIM, (h + 1) * HEAD_DIM)
        first.append(jnp.where(lane < QK_DIM, y[:, hs], aug[:, hs]))
        second.append(jnp.where(lane < QK_DIM, pltpu.roll(y[:, hs], QK_DIM, 1), aug[:, hs]))
    return jnp.concatenate(first, axis=1), jnp.concatenate(second, axis=1)


def _aug_tables(tm, heads):
    q_tab = np.zeros((tm, heads, HEAD_DIM), np.float32)
    q_tab[:, :, QK_DIM:QK_DIM + ALIBI_LANES] = 1.0
    k_tab = np.zeros((tm, heads, HEAD_DIM), np.float32)
    slope = LOG2E * 2.0 ** (-8.0 * np.arange(1, heads + 1, dtype=np.float32) / heads)
    rest = (np.arange(tm, dtype=np.float32)[:, None] * slope[None, :]).astype(np.float32)
    for term in range(ALIBI_LANES):
        part = rest.astype(jnp.bfloat16).astype(np.float32)
        k_tab[:, :, QK_DIM + term] = part
        rest = rest - part
    return q_tab.reshape(tm, heads * HEAD_DIM), k_tab.reshape(tm, heads * HEAD_DIM)


def _even_in_proj(h, w_in, q_g, k_g, split_maps):
    m = h.shape[0]
    wdt = w_in.shape[1] // 6
    heads = wdt // HEAD_DIM
    tm = min(ATT_BLOCK, m)
    s_mat = _half_mean_matrix()
    row = lambda n, i: (i, 0)
    fix = lambda n, i: (0, 0)
    f32_out = ((m, wdt), F32, (tm, wdt), row)
    bf_out = ((m, wdt), BF16, (tm, wdt), row)
    qg = jnp.tile(q_g, heads).reshape(1, wdt)
    kg = jnp.tile(k_g, heads).reshape(1, wdt)
    q_tab, k_tab = _aug_tables(tm, heads)
    norm_extras = lambda g, tab: ([(g, (1, wdt), fix), (s_mat, (HEAD_DIM, HEAD_DIM), fix)]
                                  + ([(jnp.asarray(tab), (tm, wdt), fix)] if split_maps else []))

    def ep_plain(acc, e, o):
        o[0][...] = acc

    def ep_silu(acc, e, o):
        o[0][...] = _silu(acc)

    def ep_q(acc, e, o):
        y = _qk_norm(acc, e[0][...], e[1][...]) * (LOG2E * QK_DIM ** -0.5)
        if split_maps:
            y1, y2 = _split_maps(y, e[2][...])
            o[0][...] = y1.astype(BF16)
            o[1][...] = y2.astype(BF16)
        else:
            o[0][...] = y.astype(BF16)

    def ep_k(acc, e, o):
        y = _qk_norm(acc, e[0][...], e[1][...])
        o[0][...] = y
        if split_maps:
            y1, y2 = _split_maps(y, e[2][...])
            o[1][...] = y1.astype(BF16)
            o[2][...] = y2.astype(BF16)

    def ep_v(acc, e, o):
        o[0][...] = acc
        if split_maps:
            o[1][...] = acc.astype(BF16)

    maps = [bf_out, bf_out] if split_maps else []
    (u,) = _linear("in_even_u", [h], w_in, 0, wdt, 1, ep_plain, [], [f32_out], tm)
    (zs,) = _linear("in_even_zs", [h], w_in, 1, wdt, 1, ep_silu, [], [f32_out], tm)
    q = _linear("in_even_q", [h], w_in, 2, wdt, 1, ep_q, norm_extras(qg, q_tab), maps or [bf_out], tm)
    k = _linear("in_even_k", [h], w_in, 3, wdt, 1, ep_k, norm_extras(kg, k_tab), [f32_out] + maps, tm)
    v = _linear("in_even_v", [h], w_in, 4, wdt, 1, ep_v, [], [f32_out] + maps[:1], tm)
    (za,) = _linear("in_even_za", [h], w_in, 5, wdt, 1, ep_silu, [], [f32_out], tm)
    return u, zs, q, k, v, za


def _glu_proj(gy, w_glu, zs_gate):
    m, wdt = gy.shape
    tm = min(512, m)

    def ep(acc, e, o):
        o[0][...] = (acc[:, :wdt] * jax.nn.sigmoid(acc[:, wdt:]) * e[0][...]).astype(BF16)

    (out,) = _linear("s5_glu", [gy], w_glu, 0, 2 * wdt, 1, ep,
                     [(zs_gate, (tm, wdt), lambda n, i: (i, 0))],
                     [((m, wdt), BF16, (tm, wdt), lambda n, i: (i, 0))])
    return out


def _out_proj(name, xs, w, x_res, mod):
    m, d = x_res.shape
    tm = min(512, m)
    tn = min(1024, d)
    ncb = d // tn
    gate_col = 2 * ncb
    if mod.shape[0] == 1:
        gate = (mod, (1, tn), lambda n, i: (0, gate_col + n))
    else:
        gate = (mod, (tm, tn), lambda n, i: (i, gate_col + n))

    def ep(acc, e, o):
        o[0][...] = e[0][...] + e[1][...] * acc

    (y,) = _linear(name, xs, w, 0, tn, ncb, ep,
                   [(x_res, (tm, tn), lambda n, i: (i, n)), gate],
                   [((m, d), F32, (tm, tn), lambda n, i: (i, n))])
    return y


def _scan_tables(a_r, a_i, coef_s):
    n = a_r.shape[-1]
    row = lax.broadcasted_iota(jnp.int32, (SUBLANES, n), 0)
    a1 = (jnp.broadcast_to(a_r, (SUBLANES, n)), jnp.broadcast_to(a_i, (SUBLANES, n)))
    cmul = lambda x, y: (x[0] * y[0] - x[1] * y[1], x[0] * y[1] + x[1] * y[0])
    a2 = cmul(a1, a1)
    a4 = cmul(a2, a2)
    pc = a1
    for k in range(1, SUBLANES):
        nxt = cmul(pc, a1)
        pc = (jnp.where(row >= k, nxt[0], pc[0]), jnp.where(row >= k, nxt[1], pc[1]))
    for idx, (p, dist) in enumerate(((a1, 1), (a2, 2), (a4, 4))):
        coef_s[2 * idx] = jnp.where(row >= dist, p[0], 0.0)
        coef_s[2 * idx + 1] = jnp.where(row >= dist, p[1], 0.0)
    coef_s[6] = pc[0]
    coef_s[7] = pc[1]


def _scan_tile(a, bb, carry, coefs):
    m1r, m1i, m2r, m2i, m4r, m4i, pcr, pci = coefs
    cr, ci = carry
    for dist, mr, mi in ((1, m1r, m1i), (2, m2r, m2i), (4, m4r, m4i)):
        sa = pltpu.roll(a, dist, 0)
        sb = pltpu.roll(bb, dist, 0)
        a, bb = a + (mr * sa - mi * sb), bb + (mr * sb + mi * sa)
    return a + (pcr * cr - pci * ci), bb + (pcr * ci + pci * cr)


def _s5_block_kernel(u_ref, h0_ref, m_ref, w_ref, v_ref, d_ref, a8r_ref, a8i_ref, gy_ref, hs_ref,
                     x_s, y_s, st_s, coef_s):
    t = pl.program_id(1)
    rows = x_s.shape[0]
    ns = S5_TILE_STATES

    @pl.when(t == 0)
    def _():
        _scan_tables(a8r_ref[...], a8i_ref[...], coef_s)
        st_s[...] = h0_ref[0]

    u8 = jnp.concatenate([u_ref[pl.ds(s, rows, stride=SUBLANES), :] for s in range(SUBLANES)], axis=1)
    ub = u8.astype(BF16)
    x_s[...] = jnp.dot(ub, w_ref[0], preferred_element_type=F32)

    coefs = [coef_s[k] for k in range(8)]
    row = lax.broadcasted_iota(jnp.int32, (SUBLANES, ns), 0)

    def tile(ti, carry):
        r0 = pl.multiple_of(ti * SUBLANES, SUBLANES)
        a, bb = _scan_tile(x_s[pl.ds(r0, SUBLANES), :ns], x_s[pl.ds(r0, SUBLANES), ns:], carry, coefs)
        x_s[pl.ds(r0, SUBLANES), :ns] = jnp.where(row == 0, carry[0], pltpu.roll(a, 1, 0))
        x_s[pl.ds(r0, SUBLANES), ns:] = jnp.where(row == 0, carry[1], pltpu.roll(bb, 1, 0))
        return a[SUBLANES - 1:SUBLANES, :], bb[SUBLANES - 1:SUBLANES, :]

    cr, ci = lax.fori_loop(0, rows // SUBLANES, tile, (st_s[:, :ns], st_s[:, ns:]))
    st_s[:, :ns] = cr
    st_s[:, ns:] = ci

    y8 = (jnp.dot(ub, m_ref[0], preferred_element_type=F32)
          + jnp.dot(x_s[...].astype(BF16), v_ref[0], preferred_element_type=F32))
    for s in range(SUBLANES):
        ls_ = slice(s * LANES, (s + 1) * LANES)
        y_s[pl.ds(s, rows, stride=SUBLANES), :] = y8[:, ls_] + d_ref[...] * u8[:, ls_]
    gy_ref[...] = jax.nn.gelu(y_s[...]).astype(BF16)

    @pl.when(t == pl.num_programs(1) - 1)
    def _():
        hs_ref[0] = st_s[...]


def _s5_blocked(u, prm):
    lb, wdt = u.shape
    nc = wdt // LANES
    ns = S5_TILE_STATES
    rows = math.gcd(S5_STEP_BLOCKS, lb // SUBLANES)
    nt = lb // (rows * SUBLANES)
    tok = pl.BlockSpec((rows * SUBLANES, LANES), lambda c, t: (t, c))
    op = lambda a: pl.BlockSpec((1,) + a.shape[1:], lambda c, t: (c, 0, 0))
    vec = lambda width: pl.BlockSpec((1, width), lambda c, t: (0, c))
    st = pl.BlockSpec((1, 1, 2 * ns), lambda c, t: (c, 0, 0))
    gy, hs = pl.pallas_call(
        _s5_block_kernel,
        grid=(nc, nt),
        in_specs=[tok, st, op(prm["m"]), op(prm["w"]), op(prm["v"]), vec(LANES), vec(ns), vec(ns)],
        out_specs=[tok, st],
        out_shape=[jax.ShapeDtypeStruct((lb, wdt), BF16), jax.ShapeDtypeStruct((nc, 1, 2 * ns), F32)],
        scratch_shapes=[pltpu.VMEM((rows, 2 * ns), F32), pltpu.VMEM((rows * SUBLANES, LANES), F32),
                        pltpu.VMEM((1, 2 * ns), F32), pltpu.VMEM((8, SUBLANES, ns), F32)],
        compiler_params=_params("arbitrary", "arbitrary"),
        name="s5_blocked",
    )(u, jnp.zeros((nc, 1, 2 * ns), F32), prm["m"], prm["w"], prm["v"], prm["d"], prm["a8r"], prm["a8i"])
    hs = hs.reshape(nc, 2, ns)
    return gy, hs[:, 0].reshape(1, 1, nc * ns), hs[:, 1].reshape(1, 1, nc * ns)


def _s5_kernel(u_ref, h0r_ref, h0i_ref, wbr_ref, wbi_ref, wcr_ref, wci_ref, d_ref, ar_ref, ai_ref,
               gy_ref, hr_ref, hi_ref, xr_s, xi_s, sr_s, si_s, coef_s, *, lane_chunk):
    b, t = pl.program_id(0), pl.program_id(1)
    rows, ns = xr_s.shape
    nc = wbr_ref.shape[0]

    @pl.when((b == 0) & (t == 0))
    def _():
        _scan_tables(ar_ref[...], ai_ref[...], coef_s)

    @pl.when(t == 0)
    def _():
        sr_s[...] = h0r_ref[0]
        si_s[...] = h0i_ref[0]

    ub = u_ref[...].astype(BF16)
    for c in range(nc):
        uc = ub[:, c * LANES:(c + 1) * LANES]
        cs = slice(c * S5_TILE_STATES, (c + 1) * S5_TILE_STATES)
        xr_s[:, cs] = jnp.dot(uc, wbr_ref[c], preferred_element_type=F32)
        xi_s[:, cs] = jnp.dot(uc, wbi_ref[c], preferred_element_type=F32)

    for lc in range(ns // lane_chunk):
        ls = slice(lc * lane_chunk, (lc + 1) * lane_chunk)
        coefs = [coef_s[k, :, ls] for k in range(8)]

        def tile(ti, carry, ls=ls, coefs=coefs):
            r0 = pl.multiple_of(ti * SUBLANES, SUBLANES)
            a, bb = _scan_tile(xr_s[pl.ds(r0, SUBLANES), ls], xi_s[pl.ds(r0, SUBLANES), ls], carry, coefs)
            xr_s[pl.ds(r0, SUBLANES), ls] = a
            xi_s[pl.ds(r0, SUBLANES), ls] = bb
            return a[SUBLANES - 1:SUBLANES, :], bb[SUBLANES - 1:SUBLANES, :]

        cr, ci = lax.fori_loop(0, rows // SUBLANES, tile, (sr_s[:, ls], si_s[:, ls]))
        sr_s[:, ls] = cr
        si_s[:, ls] = ci

    for c in range(nc):
        cs = slice(c * S5_TILE_STATES, (c + 1) * S5_TILE_STATES)
        os_ = slice(c * LANES, (c + 1) * LANES)
        y = (jnp.dot(xr_s[:, cs].astype(BF16), wcr_ref[c], preferred_element_type=F32)
             - jnp.dot(xi_s[:, cs].astype(BF16), wci_ref[c], preferred_element_type=F32)
             + d_ref[:, os_] * u_ref[:, os_])
        gy_ref[:, os_] = jax.nn.gelu(y).astype(BF16)

    @pl.when(t == pl.num_programs(1) - 1)
    def _():
        hr_ref[0] = sr_s[...]
        hi_ref[0] = si_s[...]


def _s5(u, h0r, h0i, prm, nb, lb):
    wdt = u.shape[1]
    ns = h0r.shape[-1]
    nc = wdt // LANES
    rows = min(256, lb)
    nt = lb // rows
    full = lambda a: pl.BlockSpec(a.shape, lambda b, t, nd=a.ndim: (0,) * nd)
    st = pl.BlockSpec((1, 1, ns), lambda b, t: (b, 0, 0))
    tok = pl.BlockSpec((rows, wdt), lambda b, t: (b * nt + t, 0))
    return pl.pallas_call(
        functools.partial(_s5_kernel, lane_chunk=min(512, ns)),
        grid=(nb, nt),
        in_specs=[tok, st, st] + [full(prm[k]) for k in ("wbr", "wbi", "wcr", "wci", "d", "ar", "ai")],
        out_specs=[tok, st, st],
        out_shape=[jax.ShapeDtypeStruct((nb * lb, wdt), BF16),
                   jax.ShapeDtypeStruct((nb, 1, ns), F32),
                   jax.ShapeDtypeStruct((nb, 1, ns), F32)],
        scratch_shapes=[pltpu.VMEM((rows, ns), F32), pltpu.VMEM((rows, ns), F32),
                        pltpu.VMEM((1, ns), F32), pltpu.VMEM((1, ns), F32),
                        pltpu.VMEM((8, SUBLANES, ns), F32)],
        compiler_params=_params("arbitrary", "arbitrary"),
        name="s5_scan",
    )(u, h0r, h0i, prm["wbr"], prm["wbi"], prm["wcr"], prm["wci"], prm["d"], prm["ar"], prm["ai"])


def _s5_params(lam_re, lam_im, log_dt, b_re, b_im, c_re, c_im, d_skip):
    g, p = lam_re.shape
    nc = g // S5_LANE_GROUPS
    dt = jnp.exp(log_dt)[:, None]
    mag = jnp.exp(lam_re * dt)
    ar = mag * jnp.cos(lam_im * dt)
    ai = mag * jnp.sin(lam_im * dt)
    den = lam_re * lam_re + lam_im * lam_im
    zr = ((ar - 1.0) * lam_re + ai * lam_im) / den
    zi = (ai * lam_re - (ar - 1.0) * lam_im) / den
    bbr = zr[..., None] * b_re - zi[..., None] * b_im
    bbi = zr[..., None] * b_im + zi[..., None] * b_re
    eye = jnp.eye(S5_LANE_GROUPS, dtype=F32)

    def bdiag_in(x):
        t = x.reshape(nc, S5_LANE_GROUPS, p, S5_GROUP).transpose(0, 1, 3, 2)
        return (t[:, :, :, None, :] * eye[None, :, None, :, None]).reshape(nc, LANES, S5_TILE_STATES).astype(BF16)

    def bdiag_out(x):
        t = x.reshape(nc, S5_LANE_GROUPS, S5_GROUP, p).transpose(0, 1, 3, 2)
        return (t[:, :, :, None, :] * eye[None, :, None, :, None]).reshape(nc, S5_TILE_STATES, LANES).astype(BF16)

    prm = dict(wbr=bdiag_in(bbr), wbi=bdiag_in(bbi), wcr=bdiag_out(c_re), wci=bdiag_out(c_im),
               d=d_skip.reshape(1, g * S5_GROUP), ar=ar.reshape(1, g * p), ai=ai.reshape(1, g * p))
    prm.update(_s5_block_operators(lam_re * dt, lam_im * dt, bbr, bbi, c_re, c_im))
    return prm


def _s5_block_operators(log_mag, angle, bbr, bbi, c_re, c_im):
    g, p = log_mag.shape
    nc = g // S5_LANE_GROUPS
    t8 = SUBLANES
    hp = lax.Precision.HIGHEST
    j = jnp.arange(t8 + 1, dtype=F32)[:, None, None]
    pw_mag = jnp.exp(j * log_mag[None])
    pw_r, pw_i = pw_mag * jnp.cos(j * angle[None]), pw_mag * jnp.sin(j * angle[None])

    def c_times(lo):
        a_r, a_i = pw_r[lo:lo + t8, :, None, :], pw_i[lo:lo + t8, :, None, :]
        return c_re[None] * a_r - c_im[None] * a_i, c_re[None] * a_i + c_im[None] * a_r

    def block_diagonal(compact, row_group, lane_group):
        rows = compact.shape[1]
        width = S5_LANE_GROUPS * LANES
        lane = jnp.arange(width)
        src = jnp.arange(LANES)
        same = ((lane // (S5_LANE_GROUPS * lane_group))[None, :] == (src // lane_group)[:, None]) \
            & ((lane % lane_group)[None, :] == (src % lane_group)[:, None])
        spread = jnp.einsum("crl,lw->crw", compact.astype(BF16), same.astype(BF16),
                            preferred_element_type=F32)
        row_i = (jnp.arange(rows) // row_group) % S5_LANE_GROUPS
        lane_j = (lane // lane_group) % S5_LANE_GROUPS
        return jnp.where(row_i[:, None] == lane_j[None, :], spread, 0.0).astype(BF16)

    e_r, e_i = c_times(0)
    k = (jnp.einsum("jgop,gpi->jgoi", e_r, bbr, precision=hp)
         - jnp.einsum("jgop,gpi->jgoi", e_i, bbi, precision=hp))
    k_pad = jnp.concatenate([jnp.zeros((t8 - 1,) + k.shape[1:], F32), k], axis=0)
    toe = jnp.stack([k_pad[t8 - 1 - s:2 * t8 - 1 - s] for s in range(t8)])
    toe = toe.reshape(t8, t8, nc, S5_LANE_GROUPS, S5_GROUP, S5_GROUP)
    m = block_diagonal(toe.transpose(2, 0, 3, 5, 1, 4).reshape(nc, t8 * LANES, LANES),
                       S5_GROUP, S5_GROUP)

    j_rev = (t8 - 1) - j[:t8]
    rev_mag = jnp.exp(j_rev * log_mag[None])
    rev_r, rev_i = rev_mag * jnp.cos(j_rev * angle[None]), rev_mag * jnp.sin(j_rev * angle[None])
    ab_r = rev_r[..., None] * bbr[None] - rev_i[..., None] * bbi[None]
    ab_i = rev_r[..., None] * bbi[None] + rev_i[..., None] * bbr[None]
    ab = jnp.stack([ab_r, ab_i]).reshape(2, t8, nc, S5_LANE_GROUPS, p, S5_GROUP)
    w = block_diagonal(ab.transpose(2, 1, 3, 5, 0, 4).reshape(nc, t8 * LANES, 2 * p), S5_GROUP, p)

    f_r, f_i = c_times(1)
    cf = jnp.stack([f_r, -f_i]).reshape(2, t8, nc, S5_LANE_GROUPS, S5_GROUP, p)
    v = block_diagonal(cf.transpose(2, 0, 3, 5, 1, 4).reshape(nc, 2 * S5_TILE_STATES, LANES),
                       p, S5_GROUP)

    return dict(m=m, w=w, v=v, a8r=pw_r[t8].reshape(1, g * p), a8i=pw_i[t8].reshape(1, g * p))


def _stack_maps(q):
    lane = lax.broadcasted_iota(jnp.int32, q.shape, 1)
    zero = jnp.zeros_like(q)
    return jnp.concatenate([jnp.where(lane < QK_DIM, q, zero), jnp.where(lane >= QK_DIM, q, zero)], axis=0)


def _qk(qq, kb):
    return lax.dot_general(qq, kb, (((1,), (1,)), ((), ())), preferred_element_type=F32)


def _attn_finish(o, g_row, gate, lam_init):
    ms = jnp.mean(o * o, axis=-1, keepdims=True)
    return (o * lax.rsqrt(ms + EPS) * g_row * (1.0 - lam_init) * gate).astype(BF16)


def _attn_kernel(slopes_ref, lam_ref, q1_ref, q2_ref, k1_ref, k2_ref, vt_ref, za_ref, og_ref, o_ref,
                 m_s, l_s, acc_s, sa_s, sb_s, ma_s, mb_s, dbias_s, *, lam_init):
    h, i = pl.program_id(0), pl.program_id(1)
    t = q1_ref.shape[0]
    nq = 2 * t
    slope = slopes_ref[h]

    @pl.when(i == 0)
    def _():
        k_local = lax.broadcasted_iota(jnp.int32, (t, nq), 0)
        q_col = lax.broadcasted_iota(jnp.int32, (t, nq), 1)
        q_local = jnp.where(q_col >= t, q_col - t, q_col)
        vis = (k_local // CHUNK) <= (q_local // CHUNK)
        dist = jnp.abs(q_local - k_local).astype(F32)
        dbias_s[...] = (jnp.where(vis, -slope * dist, NEG_INF)
                        + slope * (q_local - k_local).astype(F32))

    q1 = q1_ref[...]
    q2 = q2_ref[...]
    m_s[...] = jnp.full(m_s.shape, -jnp.inf, F32)
    l_s[...] = jnp.zeros(l_s.shape, F32)
    acc_s[...] = jnp.zeros(acc_s.shape, F32)

    def scores(blk, diagonal, s_ref, mx_ref):
        r0 = pl.multiple_of(blk * t, t)
        for half, (k_ref, q) in enumerate(((k1_ref, q1), (k2_ref, q2))):
            cols = slice(half * t, (half + 1) * t)
            s_t = _qk(k_ref[pl.ds(r0, t), :], q)
            if diagonal:
                s_t = s_t + dbias_s[:, cols]
            s_ref[:, cols] = s_t
            mx_ref[:, cols] = jnp.max(s_t, axis=0, keepdims=True)

    def consume(blk, s_ref, mx_ref):
        r0 = pl.multiple_of(blk * t, t)
        shift = slope * ((blk - i) * t).astype(F32)
        m_old = m_s[...]
        m_new = jnp.maximum(m_old, mx_ref[...] + shift)
        alpha = jnp.exp2(m_old - m_new)
        p = jnp.exp2(s_ref[...] - (m_new - shift))
        l_s[...] = alpha * l_s[...] + jnp.sum(p, axis=0, keepdims=True)
        acc_s[...] = alpha * acc_s[...] + jnp.dot(vt_ref[:, pl.ds(r0, t)], p.astype(BF16),
                                                  preferred_element_type=F32)
        m_s[...] = m_new

    scores(i, True, sa_s, ma_s)

    def two_blocks(j0):
        scores(j0, False, sb_s, mb_s)
        consume(jnp.where(j0 == 0, i, j0 - 1), sa_s, ma_s)
        scores(j0 + 1, False, sa_s, ma_s)
        consume(j0, sb_s, mb_s)

    def eight_blocks(jj, carry):
        for pair in range(4):
            two_blocks(8 * jj + 2 * pair)
        return carry

    eights = i // 8
    lax.fori_loop(0, eights, eight_blocks, 0)

    @pl.when(i % 8 >= 4)
    def _():
        two_blocks(8 * eights)
        two_blocks(8 * eights + 2)

    @pl.when(i % 4 >= 2)
    def _():
        two_blocks(i - i % 4)

    paired = i - i % 2
    in_a = jnp.where(paired == 0, i, paired - 1)

    @pl.when(i % 2 == 1)
    def _():
        scores(i - 1, False, sb_s, mb_s)
        consume(in_a, sa_s, ma_s)
        consume(i - 1, sb_s, mb_s)

    @pl.when(i % 2 == 0)
    def _():
        consume(in_a, sa_s, ma_s)

    acc = acc_s[...]
    l = l_s[...]
    o_t = acc[:, :t] / l[:, :t] - lam_ref[0] * (acc[:, t:] / l[:, t:])
    ms = jnp.mean(o_t * o_t, axis=0, keepdims=True)
    y = (o_t * lax.rsqrt(ms + EPS)).T
    o_ref[...] = (y * og_ref[...] * (1.0 - lam_init) * za_ref[...]).astype(BF16)


def _attention_prompt(q1, q2, k1, k2, vb, za_gate, out_g, lam, lam_init):
    m, wdt = q1.shape
    heads = wdt // HEAD_DIM
    t = min(ATT_BLOCK, m)
    slopes = LOG2E * 2.0 ** (-8.0 * jnp.arange(1, heads + 1, dtype=F32) / heads)
    smem = pl.BlockSpec(memory_space=pltpu.SMEM)
    blk = pl.BlockSpec((t, HEAD_DIM), lambda h, i: (i, h))
    keys = pl.BlockSpec((m, HEAD_DIM), lambda h, i: (0, h))
    return pl.pallas_call(
        functools.partial(_attn_kernel, lam_init=lam_init),
        grid=(heads, m // t),
        in_specs=[smem, smem, blk, blk, keys, keys,
                  pl.BlockSpec((HEAD_DIM, m), lambda h, i: (h, 0)),
                  blk, pl.BlockSpec((1, HEAD_DIM), lambda h, i: (0, 0))],
        out_specs=blk,
        out_shape=jax.ShapeDtypeStruct((m, wdt), BF16),
        scratch_shapes=[pltpu.VMEM((1, 2 * t), F32), pltpu.VMEM((1, 2 * t), F32),
                        pltpu.VMEM((HEAD_DIM, 2 * t), F32),
                        pltpu.VMEM((t, 2 * t), F32), pltpu.VMEM((t, 2 * t), F32),
                        pltpu.VMEM((1, 2 * t), F32), pltpu.VMEM((1, 2 * t), F32),
                        pltpu.VMEM((t, 2 * t), F32)],
        compiler_params=_params("arbitrary", "arbitrary"),
        name="attn_prompt",
    )(slopes, lam, q1, q2, k1, k2, vb.T, za_gate, out_g.reshape(1, HEAD_DIM))


MASKED_DIST = 1e32


def _attn_cached_kernel(lam_ref, q_ref, kn_ref, vn_ref, kc_ref, vc_ref, za_ref, og_ref, o_ref,
                        dc_s, dn_s, *, lam_init, heads):
    ls = q_ref.shape[0]
    past = kc_ref.shape[2]
    lam = lam_ref[0]

    @pl.when(pl.program_id(0) == 0)
    def _():
        def table(n, k0):
            q_row = lax.broadcasted_iota(jnp.int32, (2 * ls, n), 0)
            q_pos = past + jnp.where(q_row >= ls, q_row - ls, q_row)
            k_pos = k0 + lax.broadcasted_iota(jnp.int32, (2 * ls, n), 1)
            vis = (k_pos // CHUNK) <= (q_pos // CHUNK)
            return jnp.where(vis, jnp.abs(q_pos - k_pos).astype(F32), MASKED_DIST)

        dc_s[...] = table(past, 0)
        dn_s[...] = table(ls, past)

    for h in range(heads):
        hs = slice(h * HEAD_DIM, (h + 1) * HEAD_DIM)
        slope = LOG2E * 2.0 ** (-8.0 * (h + 1) / heads)
        qq = _stack_maps(q_ref[:, hs])
        sc = _qk(qq, kc_ref[0, h]) - slope * dc_s[...]
        sn = _qk(qq, kn_ref[:, hs].astype(BF16)) - slope * dn_s[...]
        mx = jnp.maximum(jnp.max(sc, axis=1, keepdims=True), jnp.max(sn, axis=1, keepdims=True))
        pc = jnp.exp2(sc - mx)
        pn = jnp.exp2(sn - mx)
        inv = 1.0 / (jnp.sum(pc, axis=1, keepdims=True) + jnp.sum(pn, axis=1, keepdims=True))
        pc = pc * inv
        pn = pn * inv
        wc = (pc[:ls] - lam * pc[ls:]).astype(BF16)
        wn = (pn[:ls] - lam * pn[ls:]).astype(BF16)
        o = (jnp.dot(wc, vc_ref[0, h], preferred_element_type=F32)
             + jnp.dot(wn, vn_ref[:, hs].astype(BF16), preferred_element_type=F32))
        o_ref[:, hs] = _attn_finish(o, og_ref[...], za_ref[:, hs], lam_init)


def _attention_cached(q, k_new, v_new, cache_k, cache_v, za_gate, out_g, lam, lam_init, nb, ls):
    wdt = q.shape[1]
    heads = wdt // HEAD_DIM
    past = cache_k.shape[1]
    head_major = lambda a: jnp.transpose(a.astype(BF16), (0, 2, 1, 3))
    tok = pl.BlockSpec((ls, wdt), lambda b: (b, 0))
    cache = pl.BlockSpec((1, heads, past, HEAD_DIM), lambda b: (b, 0, 0, 0))
    return pl.pallas_call(
        functools.partial(_attn_cached_kernel, lam_init=lam_init, heads=heads),
        grid=(nb,),
        in_specs=[pl.BlockSpec(memory_space=pltpu.SMEM), tok, tok, tok, cache, cache, tok,
                  pl.BlockSpec((1, HEAD_DIM), lambda b: (0, 0))],
        out_specs=tok,
        out_shape=jax.ShapeDtypeStruct((nb * ls, wdt), BF16),
        scratch_shapes=[pltpu.VMEM((2 * ls, past), F32), pltpu.VMEM((2 * ls, ls), F32)],
        compiler_params=_params("arbitrary"),
        name="attn_cached",
    )(lam, q, k_new, v_new, head_major(cache_k), head_major(cache_v), za_gate,
      out_g.reshape(1, HEAD_DIM))


def _odd_in_proj(h, w_in):
    m, d = h.shape
    tm = min(512, m)
    tn = min(512, d)
    ncb = d // tn

    def kernel(h_ref, wa_ref, wb_ref, wz_ref, g_ref, z_ref, w_s):
        @pl.when(pl.program_id(1) == 0)
        def _():
            for k, w_ref in enumerate((wa_ref, wb_ref, wz_ref)):
                w_s[k] = w_ref[...].astype(BF16)

        x = h_ref[...]
        a = jnp.dot(x, w_s[0], preferred_element_type=F32)
        b = jnp.dot(x, w_s[1], preferred_element_type=F32)
        z = jnp.dot(x, w_s[2], preferred_element_type=F32)
        g_ref[...] = a * jax.nn.sigmoid(b)
        z_ref[...] = _silu(z)

    wspec = lambda off: pl.BlockSpec((d, tn), lambda n, i, off=off: (0, n + off))
    out = pl.BlockSpec((tm, tn), lambda n, i: (i, n))
    return pl.pallas_call(
        kernel,
        grid=(ncb, m // tm),
        in_specs=[pl.BlockSpec((tm, d), lambda n, i: (i, 0)), wspec(0), wspec(ncb), wspec(2 * ncb)],
        out_specs=[out, out],
        out_shape=[jax.ShapeDtypeStruct((m, d), F32), jax.ShapeDtypeStruct((m, d), F32)],
        scratch_shapes=[pltpu.VMEM((3, d, tn), BF16)],
        compiler_params=_params("arbitrary", "arbitrary"),
        name="in_odd",
    )(h, w_in, w_in, w_in)


CONV_HALO = 32


LN_ROWS = 64


def _conv_kernel(cur_ref, prev_ref, ctx_ref, zg_ref, w_ref, b_ref, lg_ref, lb_ref, o_ref, xp_s, y_s,
                 sh_s, *, row_chunk):
    t = pl.program_id(1)
    tm, c = cur_ref.shape

    @pl.when(t == 0)
    def _():
        xp_s[0:CONV_HALO, :] = ctx_ref[0]

    @pl.when(t > 0)
    def _():
        xp_s[0:CONV_HALO, :] = prev_ref[...]

    xp_s[CONV_HALO:CONV_HALO + tm, :] = cur_ref[...]
    xp_s[CONV_HALO + tm:, :] = jnp.zeros((SUBLANES, c), F32)

    first = CONV_HALO - (CONV_W - 1)
    span = row_chunk + CONV_HALO

    def lane_body(lc, carry):
        c0 = pl.multiple_of(lc * LANES, LANES)

        def row_body(rc, carry2):
            r0 = pl.multiple_of(rc * row_chunk, row_chunk)
            blk = xp_s[pl.ds(r0, span + SUBLANES), pl.ds(c0, LANES)]
            for phase in range(SUBLANES):
                sh_s[phase] = blk[phase:phase + span]
            acc = jnp.zeros((row_chunk, LANES), F32)
            for phase in range(SUBLANES):
                for a in range(span // SUBLANES):
                    tap = SUBLANES * a + phase - first
                    if 0 <= tap < CONV_W:
                        acc = acc + (w_ref[pl.ds(tap, 1), pl.ds(c0, LANES)]
                                     * sh_s[phase, SUBLANES * a:SUBLANES * a + row_chunk, :])
            y_s[pl.ds(r0, row_chunk), pl.ds(c0, LANES)] = acc
            return carry2

        return lax.fori_loop(0, tm // row_chunk, row_body, carry)

    lax.fori_loop(0, c // LANES, lane_body, 0)

    ln_rows = min(LN_ROWS, tm)

    def ln_body(r, carry):
        rows = pl.ds(pl.multiple_of(r * ln_rows, ln_rows), ln_rows)
        y = y_s[rows, :] + b_ref[...]
        mu = jnp.mean(y, axis=-1, keepdims=True)
        yc = y - mu
        var = jnp.mean(yc * yc, axis=-1, keepdims=True)
        yn = yc * lax.rsqrt(var + EPS) * lg_ref[...] + lb_ref[...]
        o_ref[rows, :] = (_silu(yn) * zg_ref[rows, :]).astype(BF16)
        return carry

    lax.fori_loop(0, tm // ln_rows, ln_body, 0, unroll=True)


def _conv_module(g, ctx, zg, conv_w, conv_b, ln_g, ln_b, nb, lb):
    c = g.shape[1]
    tm = min(256, lb)
    nt = lb // tm
    halo_blocks = tm // CONV_HALO
    w_pad = jnp.zeros((CONV_HALO, c), F32).at[:CONV_W].set(conv_w)
    cur = pl.BlockSpec((tm, c), lambda b, t: (b * nt + t, 0))
    prev = pl.BlockSpec((CONV_HALO, c), lambda b, t: (jnp.maximum((b * nt + t) * halo_blocks - 1, 0), 0))
    vec = pl.BlockSpec((1, c), lambda b, t: (0, 0))
    return pl.pallas_call(
        functools.partial(_conv_kernel, row_chunk=min(64, tm)),
        grid=(nb, nt),
        in_specs=[cur, prev, pl.BlockSpec((1, CONV_HALO, c), lambda b, t: (b, 0, 0)), cur,
                  pl.BlockSpec((CONV_HALO, c), lambda b, t: (0, 0)), vec, vec, vec],
        out_specs=cur,
        out_shape=jax.ShapeDtypeStruct((nb * lb, c), BF16),
        scratch_shapes=[pltpu.VMEM((CONV_HALO + tm + SUBLANES, c), F32), pltpu.VMEM((tm, c), F32),
                        pltpu.VMEM((SUBLANES, min(64, tm) + CONV_HALO, LANES), F32)],
        compiler_params=_params("arbitrary", "arbitrary"),
        name="conv_module",
    )(g, g, ctx, zg, w_pad, conv_b.reshape(1, c), ln_g.reshape(1, c), ln_b.reshape(1, c))


def kernel(x_prompt, x_sample, c_prompt, c_sample, cache_k, cache_v, state_s5_re, state_s5_im, state_conv, norm_g, w_ada, b_ada, w_in_even, w_out_even, s5_lam_re, s5_lam_im, s5_log_dt, s5_b_re, s5_b_im, s5_c_re, s5_c_im, s5_d, s5_w_glu, q_norm_g, k_norm_g, lam_q1, lam_k1, lam_q2, lam_k2, attn_out_g, w_in_odd, conv_w, conv_b, conv_ln_g, conv_ln_b, w_out_odd):
    bp, lp, d = x_prompt.shape
    bs, ls, _ = x_sample.shape
    assert bp == 1 and w_ada.shape[0] == 2, "one prompt sequence, one even and one odd layer"
    wdt = d // 2
    heads = wdt // HEAD_DIM
    ns = (wdt // S5_GROUP) * S5_STATE
    ms = bs * ls

    xp = x_prompt.reshape(lp, d)
    xs = x_sample.reshape(ms, d)

    rows = bp + bs
    rows_pad = -(-rows // 16) * 16
    c_all = jnp.concatenate([c_prompt, c_sample, jnp.zeros((rows_pad - rows, d), F32)], axis=0)
    mod = _ada_mod(c_all, w_ada, b_ada)
    mod_p = [mod[l, :bp] for l in range(2)]
    mod_s = [jnp.repeat(mod[l, bp:rows], ls, axis=0) for l in range(2)]

    w_in_e, w_out_e, w_glu = w_in_even[0], w_out_even[0], s5_w_glu[0]
    w_in_o, w_out_o = w_in_odd[0], w_out_odd[0]

    s5p = _s5_params(s5_lam_re[0], s5_lam_im[0], s5_log_dt[0], s5_b_re[0], s5_b_im[0],
                     s5_c_re[0], s5_c_im[0], s5_d[0])
    lam_init = 0.8 - 0.6 * math.exp(-0.3 * 0)
    lam = (jnp.exp(jnp.sum(lam_q1[0] * lam_k1[0])) - jnp.exp(jnp.sum(lam_q2[0] * lam_k2[0]))
           + lam_init).reshape(1).astype(F32)

    def even_layer(x, modl, nb, lb, h0r, h0i, cache):
        h = _modnorm(x, modl, norm_g[0])
        u, zs, q, k, v, za = _even_in_proj(h, w_in_e, q_norm_g[0], k_norm_g[0], cache is None)
        gy, hr, hi = _s5_blocked(u, s5p) if cache is None else _s5(u, h0r, h0i, s5p, nb, lb)
        s5_out = _glu_proj(gy, w_glu, zs)
        if cache is None:
            att = _attention_prompt(q[0], q[1], k[1], k[2], v[1], za, attn_out_g[0], lam, lam_init)
        else:
            att = _attention_cached(q[0], k[0], v[0], cache[0], cache[1], za, attn_out_g[0], lam,
                                    lam_init, nb, lb)
        y = _out_proj("out_even", [s5_out, att], w_out_e, x, modl)
        return y, k[0], v[0], hr, hi

    def odd_layer(x, modl, nb, lb, ctx):
        h = _modnorm(x, modl, norm_g[1])
        g, zg = _odd_in_proj(h, w_in_o)
        yc = _conv_module(g, ctx, zg, conv_w[0], conv_b[0], conv_ln_g[0], conv_ln_b[0], nb, lb)
        y = _out_proj("out_odd", [yc], w_out_o, x, modl)
        return y, g

    zero_state = jnp.zeros((bp, 1, ns), F32)
    yp, kp, vp, hrp, hip = even_layer(xp, mod_p[0], bp, lp, zero_state, zero_state, None)
    ys, ksn, vsn, hrs, his = even_layer(
        xs, mod_s[0], bs, ls, state_s5_re[0].reshape(bs, 1, ns), state_s5_im[0].reshape(bs, 1, ns),
        (cache_k[0], cache_v[0]))

    pad = CONV_HALO - (CONV_W - 1)
    ctx_p = jnp.zeros((bp, CONV_HALO, d), F32)
    ctx_s = jnp.pad(state_conv[0], ((0, 0), (pad, 0), (0, 0)))
    yp, gp = odd_layer(yp, mod_p[1], bp, lp, ctx_p)
    ys, gs = odd_layer(ys, mod_s[1], bs, ls, ctx_s)

    groups = wdt // S5_GROUP
    tail = CONV_W - 1
    conv_p = gp.reshape(bp, lp, d)[:, lp - tail:]
    conv_s = jnp.concatenate([state_conv[0], gs.reshape(bs, ls, d)], axis=1)[:, -tail:]
    return (yp.reshape(bp, lp, d), ys.reshape(bs, ls, d),
            kp.reshape(1, bp, lp, heads, HEAD_DIM), vp.reshape(1, bp, lp, heads, HEAD_DIM),
            hrp.reshape(1, bp, groups, S5_STATE), hip.reshape(1, bp, groups, S5_STATE), conv_p[None],
            ksn.reshape(1, bs, ls, heads, HEAD_DIM), vsn.reshape(1, bs, ls, heads, HEAD_DIM),
            hrs.reshape(1, bs, groups, S5_STATE), his.reshape(1, bs, groups, S5_STATE), conv_s[None])
```
